```python
import math
import jax, jax.numpy as jnp
from jax import lax
import numpy as np

D_MODEL = 2048
BATCH = 2
SEQ = 16384
DEPTH = 2

N_MIXERS = 2
N_SSM_LAYERS = (DEPTH + 1) // 2
N_RET_LAYERS = DEPTH // 2
NORM_EPS = 1e-6
CHUNK = 128

SSM_EXPAND = 2
SSM_D_INNER = SSM_EXPAND * D_MODEL
SSM_HEADDIM = 64
SSM_HEADS = SSM_D_INNER // SSM_HEADDIM
SSM_D_STATE = 128
SSM_GROUPS = 8
SSM_HPG = SSM_HEADS // SSM_GROUPS
SSM_CONV = 4
SSM_GN = SSM_GROUPS * SSM_D_STATE
SSM_CONV_DIM = SSM_D_INNER + 2 * SSM_GN
SSM_IN_DIM = SSM_D_INNER + SSM_CONV_DIM + SSM_HEADS

RET_HEADS = 8
RET_QK_DIM = D_MODEL
RET_V_DIM = 2 * D_MODEL
RET_DK = RET_QK_DIM // RET_HEADS
RET_DV = RET_V_DIM // RET_HEADS
RET_IN_DIM = 2 * RET_QK_DIM + 2 * RET_V_DIM
ROPE_BASE = 10000.0

FFN_HIDDEN = -(-(8 * D_MODEL) // (3 * 256)) * 256

kernel_name = "hybrid_mamba2_retention_swiglu"


def _rms(x, w):
    x32 = x.astype(jnp.float32)
    y = x32 * lax.rsqrt(jnp.mean(x32 * x32, axis=-1, keepdims=True) + NORM_EPS)
    return (y * w.astype(jnp.float32)).astype(x.dtype)


def _to_chunks(t, n_chunks):
    b = t.shape[0]
    return jnp.moveaxis(t.reshape((b, n_chunks, CHUNK) + t.shape[2:]), 1, 0)


def _from_chunks(t):
    t = jnp.moveaxis(t, 0, 1)
    return t.reshape((t.shape[0], t.shape[1] * t.shape[2]) + t.shape[3:])


def _causal_dwconv(x, w, b):
    c = x.shape[-1]
    y = lax.conv_general_dilated(x, w[:, None, :].astype(x.dtype), window_strides=(1,),
                                 padding=[(SSM_CONV - 1, 0)],
                                 dimension_numbers=('NWC', 'WIO', 'NWC'),
                                 feature_group_count=c)
    return y + b.astype(x.dtype)


def _ssd_scan(xh, dt, a_neg, bm, cm):
    f32 = jnp.float32
    bsz, s, _, p = xh.shape
    nc = s // CHUNK
    a = dt * a_neg
    xs = (_to_chunks(xh.astype(f32).reshape(bsz, s, SSM_GROUPS, SSM_HPG, p), nc),
          _to_chunks(dt.reshape(bsz, s, SSM_GROUPS, SSM_HPG), nc),
          _to_chunks(a.reshape(bsz, s, SSM_GROUPS, SSM_HPG), nc),
          _to_chunks(bm.astype(f32), nc),
          _to_chunks(cm.astype(f32), nc))
    causal = jnp.tril(jnp.ones((CHUNK, CHUNK), dtype=bool))[None, :, :, None, None]

    def step(h, inp):
        x_c, dt_c, a_c, b_c, c_c = inp
        cum = jnp.cumsum(a_c, axis=1)
        seg = cum[:, :, None] - cum[:, None, :]
        decay = jnp.exp(jnp.where(causal, seg, -jnp.inf))
        cb = jnp.einsum('bign,bjgn->bijg', c_c, b_c)
        wgt = decay * cb[..., None] * dt_c[:, None]
        y = jnp.einsum('bijgh,bjghp->bighp', wgt, x_c)
        y = y + jnp.einsum('bign,bghpn->bighp', c_c, h) * jnp.exp(cum)[..., None]
        to_end = jnp.exp(cum[:, -1:] - cum) * dt_c
        h = h * jnp.exp(cum[:, -1])[..., None, None] + jnp.einsum('bjgh,bjghp,bjgn->bghpn', to_end, x_c, b_c)
        return h, y

    h0 = jnp.zeros((bsz, SSM_GROUPS, SSM_HPG, p, SSM_D_STATE), f32)
    _, ys = lax.scan(step, h0, xs)
    return _from_chunks(ys).reshape(bsz, s, SSM_HEADS, p)


def _mamba2_mixer(h, w_in, conv_w, conv_b, dt_bias, a_log, d_skip, norm_w, w_out):
    f32 = jnp.float32
    bsz, s, _ = h.shape
    zxbcdt = h @ w_in
    z = zxbcdt[..., :SSM_D_INNER]
    xbc = zxbcdt[..., SSM_D_INNER:SSM_D_INNER + SSM_CONV_DIM]
    dt = zxbcdt[..., SSM_D_INNER + SSM_CONV_DIM:]
    xbc = jax.nn.silu(_causal_dwconv(xbc, conv_w, conv_b))
    xs = xbc[..., :SSM_D_INNER].reshape(bsz, s, SSM_HEADS, SSM_HEADDIM)
    bm = xbc[..., SSM_D_INNER:SSM_D_INNER + SSM_GN].reshape(bsz, s, SSM_GROUPS, SSM_D_STATE)
    cm = xbc[..., SSM_D_INNER + SSM_GN:].reshape(bsz, s, SSM_GROUPS, SSM_D_STATE)
    dt = jax.nn.softplus(dt.astype(f32) + dt_bias.astype(f32))
    a_neg = -jnp.exp(a_log.astype(f32))
    y = _ssd_scan(xs, dt, a_neg, bm, cm)
    y = y + d_skip.astype(f32)[:, None] * xs.astype(f32)
    y = y.reshape(bsz, s, SSM_D_INNER) * jax.nn.silu(z.astype(f32))
    yg = y.reshape(bsz, s, SSM_GROUPS, SSM_D_INNER // SSM_GROUPS)
    yg = yg * lax.rsqrt(jnp.mean(yg * yg, axis=-1, keepdims=True) + NORM_EPS)
    y = yg.reshape(bsz, s, SSM_D_INNER) * norm_w.astype(f32)
    return y.astype(h.dtype) @ w_out


def _rotary(t, pos):
    d = t.shape[-1]
    freq = 1.0 / (ROPE_BASE ** jnp.linspace(0.0, 1.0, d // 2, dtype=jnp.float32))
    ang = pos[:, None] * freq[None, :]
    cos = jnp.cos(ang)[None, :, None, :]
    sin = jnp.sin(ang)[None, :, None, :]
    t1, t2 = t[..., 0::2], t[..., 1::2]
    return jnp.stack([t1 * cos - t2 * sin, t1 * sin + t2 * cos], axis=-1).reshape(t.shape)


def _retention_scan(q, k, v, log_gamma):
    f32 = jnp.float32
    bsz, s, nh, dk = q.shape
    dv = v.shape[-1]
    nc = s // CHUNK
    idx = jnp.arange(CHUNK, dtype=f32)
    diff = idx[:, None] - idx[None, :]
    dmask = jnp.exp(jnp.where(diff[None] >= 0, diff[None] * log_gamma[:, None, None], -jnp.inf))
    q_decay = jnp.exp((idx[:, None] + 1.0) * log_gamma[None, :])
    k_decay = jnp.exp((CHUNK - 1.0 - idx)[:, None] * log_gamma[None, :])
    chunk_decay = jnp.exp(CHUNK * log_gamma)

    def step(st, inp):
        q_c, k_c, v_c = inp
        att = jnp.einsum('bihd,bjhd->bhij', q_c, k_c) * dmask[None]
        o = jnp.einsum('bhij,bjhe->bihe', att, v_c)
        o = o + jnp.einsum('bihd,bhde->bihe', q_c, st) * q_decay[None, :, :, None]
        st = st * chunk_decay[None, :, None, None] + jnp.einsum('bjhd,bjhe->bhde', k_c * k_decay[None, :, :, None], v_c)
        return st, o

    s0 = jnp.zeros((bsz, nh, dk, dv), f32)
    _, os_ = lax.scan(step, s0, (_to_chunks(q, nc), _to_chunks(k, nc), _to_chunks(v, nc)))
    return _from_chunks(os_)


def _retention_mixer(h, w_in, norm_w, w_out):
    f32 = jnp.float32
    bsz, s, _ = h.shape
    proj = h @ w_in
    q = proj[..., :RET_QK_DIM].astype(f32).reshape(bsz, s, RET_HEADS, RET_DK)
    k = proj[..., RET_QK_DIM:2 * RET_QK_DIM].astype(f32).reshape(bsz, s, RET_HEADS, RET_DK)
    v = proj[..., 2 * RET_QK_DIM:2 * RET_QK_DIM + RET_V_DIM].astype(f32).reshape(bsz, s, RET_HEADS, RET_DV)
    g = proj[..., 2 * RET_QK_DIM + RET_V_DIM:].astype(f32)
    pos = jnp.arange(s, dtype=f32)
    q = _rotary(q, pos)
    k = _rotary(k, pos) * (RET_DK ** -0.5)
    log_gamma = jnp.log(1.0 - 2.0 ** (-5.0 - jnp.arange(RET_HEADS, dtype=f32)))
    o = _retention_scan(q, k, v, log_gamma)
    o = o * lax.rsqrt(jnp.mean(o * o, axis=-1, keepdims=True) + NORM_EPS)
    o = o.reshape(bsz, s, RET_V_DIM) * norm_w.astype(f32)
    return (jax.nn.silu(g) * o).astype(h.dtype) @ w_out


def _swiglu(h, w_gate, w_up, w_down):
    return (jax.nn.silu(h @ w_gate) * (h @ w_up)) @ w_down


def setup_inputs(seed: int = 0) -> dict:
    key = jax.random.key(seed)
    ks = jax.random.split(key, 20)
    f32 = jnp.float32
    nrm = lambda k, shape, fan_in: jax.random.normal(k, shape, f32) * (fan_in ** -0.5)
    gain = lambda k, shape: 1.0 + 0.01 * jax.random.normal(k, shape, f32)
    x = jax.random.normal(ks[0], (BATCH, SEQ, D_MODEL), f32)
    norm_mix = gain(ks[1], (DEPTH, D_MODEL))
    ssm_w_in = nrm(ks[2], (N_SSM_LAYERS, D_MODEL, SSM_IN_DIM), D_MODEL)
    ssm_conv_w = nrm(ks[3], (N_SSM_LAYERS, SSM_CONV, SSM_CONV_DIM), SSM_CONV)
    ssm_conv_b = 0.01 * jax.random.normal(ks[4], (N_SSM_LAYERS, SSM_CONV_DIM), f32)
    dt0 = jnp.exp(jax.random.uniform(ks[5], (N_SSM_LAYERS, SSM_HEADS), f32)
                  * (math.log(0.1) - math.log(0.001)) + math.log(0.001))
    ssm_dt_bias = dt0 + jnp.log(-jnp.expm1(-dt0))
    ssm_a_log = jnp.log(jax.random.uniform(ks[6], (N_SSM_LAYERS, SSM_HEADS), f32, minval=1.0, maxval=16.0))
    ssm_d = gain(ks[7], (N_SSM_LAYERS, SSM_HEADS))
    ssm_norm = gain(ks[8], (N_SSM_LAYERS, SSM_D_INNER))
    ssm_w_out = nrm(ks[9], (N_SSM_LAYERS, SSM_D_INNER, D_MODEL), SSM_D_INNER)
    ret_w_in = nrm(ks[10], (N_RET_LAYERS, D_MODEL, RET_IN_DIM), D_MODEL)
    ret_norm = gain(ks[11], (N_RET_LAYERS, RET_V_DIM))
    ret_w_out = nrm(ks[12], (N_RET_LAYERS, RET_V_DIM, D_MODEL), RET_V_DIM)
    norm_ffn = gain(ks[13], (DEPTH, D_MODEL))
    ffn_w_gate = nrm(ks[14], (DEPTH, D_MODEL, FFN_HIDDEN), D_MODEL)
    ffn_w_up = nrm(ks[15], (DEPTH, D_MODEL, FFN_HIDDEN), D_MODEL)
    ffn_w_down = nrm(ks[16], (DEPTH, FFN_HIDDEN, D_MODEL), FFN_HIDDEN)
    norm_final = gain(ks[17], (D_MODEL,))
    return {"x": x, "norm_mix": norm_mix, "ssm_w_in": ssm_w_in, "ssm_conv_w": ssm_conv_w,
            "ssm_conv_b": ssm_conv_b, "ssm_dt_bias": ssm_dt_bias, "ssm_a_log": ssm_a_log,
            "ssm_d": ssm_d, "ssm_norm": ssm_norm, "ssm_w_out": ssm_w_out, "ret_w_in": ret_w_in,
            "ret_norm": ret_norm, "ret_w_out": ret_w_out, "norm_ffn": norm_ffn,
            "ffn_w_gate": ffn_w_gate, "ffn_w_up": ffn_w_up, "ffn_w_down": ffn_w_down,
            "norm_final": norm_final}


def reference(x, norm_mix, ssm_w_in, ssm_conv_w, ssm_conv_b, ssm_dt_bias, ssm_a_log, ssm_d,
              ssm_norm, ssm_w_out, ret_w_in, ret_norm, ret_w_out, norm_ffn, ffn_w_gate,
              ffn_w_up, ffn_w_down, norm_final):
    for i in range(DEPTH):
        h = _rms(x, norm_mix[i])
        j = i // N_MIXERS
        if i % N_MIXERS == 0:
            mix = _mamba2_mixer(h, ssm_w_in[j], ssm_conv_w[j], ssm_conv_b[j], ssm_dt_bias[j],
                                ssm_a_log[j], ssm_d[j], ssm_norm[j], ssm_w_out[j])
        else:
            mix = _retention_mixer(h, ret_w_in[j], ret_norm[j], ret_w_out[j])
        x = x + mix
        h = _rms(x, norm_ffn[i])
        x = x + _swiglu(h, ffn_w_gate[i], ffn_w_up[i], ffn_w_down[i])
    return _rms(x, norm_final)
```

```python
import functools
import math

import jax
import jax.numpy as jnp
from jax import lax
from jax.experimental import pallas as pl
from jax.experimental.pallas import tpu as pltpu

F32 = jnp.float32
BF16 = jnp.bfloat16

D_MODEL = 2048
NORM_EPS = 1e-6
CHUNK = 128

SSM_D_INNER = 4096
SSM_HEADDIM = 64
SSM_HEADS = 64
SSM_D_STATE = 128
SSM_GROUPS = 8
SSM_HPG = 8
SSM_CONV = 4
SSM_GN = SSM_GROUPS * SSM_D_STATE
SSM_CONV_DIM = SSM_D_INNER + 2 * SSM_GN
SSM_GROUP_W = SSM_HPG * SSM_HEADDIM
CONV_TAIL = 8

RET_HEADS = 8
RET_QK_DIM = 2048
RET_V_DIM = 4096
RET_DK = 256
RET_DV = 512
ROPE_BASE = 10000.0

LANES = 128
VMEM_LIMIT = 56 * 1024 * 1024


def _params(*sem):
    return pltpu.CompilerParams(dimension_semantics=sem, vmem_limit_bytes=VMEM_LIMIT)


def _silu(x):
    return x * jax.nn.sigmoid(x)


def _rms_rows(x, w):
    ms = jnp.mean(x * x, axis=-1, keepdims=True)
    return x * lax.rsqrt(ms + NORM_EPS) * w


def _rms_matmul_kernel(x_ref, nw_ref, w_ref, o_ref, h_ref):
    @pl.when(pl.program_id(1) == 0)
    def _():
        h_ref[...] = _rms_rows(x_ref[...], nw_ref[...]).astype(BF16)

    o_ref[...] = jnp.dot(h_ref[...], w_ref[...], preferred_element_type=F32).astype(o_ref.dtype)


def _rms_matmul(x, nw, w, out_dtype, tm, tn, name):
    m, k = x.shape
    n = w.shape[1]
    return pl.pallas_call(
        _rms_matmul_kernel,
        grid=(m // tm, n // tn),
        in_specs=[pl.BlockSpec((tm, k), lambda i, j: (i, 0)),
                  pl.BlockSpec((1, k), lambda i, j: (0, 0)),
                  pl.BlockSpec((k, tn), lambda i, j: (0, j))],
        out_specs=pl.BlockSpec((tm, tn), lambda i, j: (i, j)),
        out_shape=jax.ShapeDtypeStruct((m, n), out_dtype),
        scratch_shapes=[pltpu.VMEM((tm, k), BF16)],
        compiler_params=_params("parallel", "arbitrary"),
        name=name,
    )(x, nw.reshape(1, k), w)


def _ffn_in_kernel(x_ref, nw_ref, wg_ref, wu_ref, o_ref, h_ref):
    @pl.when(pl.program_id(1) == 0)
    def _():
        h_ref[...] = _rms_rows(x_ref[...], nw_ref[...]).astype(BF16)

    h = h_ref[...]
    g = jnp.dot(h, wg_ref[...], preferred_element_type=F32)
    u = jnp.dot(h, wu_ref[...], preferred_element_type=F32)
    o_ref[...] = (_silu(g) * u).astype(o_ref.dtype)


def _ffn_in(x, nw, wg, wu, tm, tn, name):
    m, k = x.shape
    n = wg.shape[1]
    return pl.pallas_call(
        _ffn_in_kernel,
        grid=(m // tm, n // tn),
        in_specs=[pl.BlockSpec((tm, k), lambda i, j: (i, 0)),
                  pl.BlockSpec((1, k), lambda i, j: (0, 0)),
                  pl.BlockSpec((k, tn), lambda i, j: (0, j)),
                  pl.BlockSpec((k, tn), lambda i, j: (0, j))],
        out_specs=pl.BlockSpec((tm, tn), lambda i, j: (i, j)),
        out_shape=jax.ShapeDtypeStruct((m, n), BF16),
        scratch_shapes=[pltpu.VMEM((tm, k), BF16)],
        compiler_params=_params("parallel", "arbitrary"),
        name=name,
    )(x, nw.reshape(1, k), wg, wu)


def _matmul_resid_kernel(a_ref, w_ref, r_ref, o_ref):
    o_ref[...] = r_ref[...] + jnp.dot(a_ref[...], w_ref[...], preferred_element_type=F32)


def _matmul_resid(a, w, resid, tm, tn, name):
    m, k = a.shape
    n = w.shape[1]
    return pl.pallas_call(
        _matmul_resid_kernel,
        grid=(m // tm, n // tn),
        in_specs=[pl.BlockSpec((tm, k), lambda i, j: (i, 0)),
                  pl.BlockSpec((k, tn), lambda i, j: (0, j)),
                  pl.BlockSpec((tm, tn), lambda i, j: (i, j))],
        out_specs=pl.BlockSpec((tm, tn), lambda i, j: (i, j)),
        out_shape=jax.ShapeDtypeStruct((m, n), F32),
        compiler_params=_params("parallel", "parallel"),
        name=name,
    )(a, w, resid)


def _rms_kernel(x_ref, nw_ref, o_ref):
    o_ref[...] = _rms_rows(x_ref[...], nw_ref[...])


def _rms(x, nw, tm, name):
    m, k = x.shape
    return pl.pallas_call(
        _rms_kernel,
        grid=(m // tm,),
        in_specs=[pl.BlockSpec((tm, k), lambda i: (i, 0)),
                  pl.BlockSpec((1, k), lambda i: (0, 0))],
        out_specs=pl.BlockSpec((tm, k), lambda i: (i, 0)),
        out_shape=jax.ShapeDtypeStruct((m, k), F32),
        compiler_params=_params("parallel"),
        name=name,
    )(x, nw.reshape(1, k))


def _ssd_kernel(zx_ref, dt_ref, cw_ref, cb_ref, dtb_ref, aneg_ref, dskip_ref, nw_ref,
                y_ref, xpad_ref, st_ref):
    q = CHUNK

    @pl.when(pl.program_id(1) == 0)
    def _():
        xpad_ref[0:CONV_TAIL, :] = jnp.zeros((CONV_TAIL, SSM_CONV_DIM), F32)
        st_ref[...] = jnp.zeros_like(st_ref)

    xpad_ref[CONV_TAIL:CONV_TAIL + q, :] = zx_ref[:, SSM_D_INNER:].astype(F32)

    row = lax.broadcasted_iota(jnp.int32, (q, q), 0)
    col = lax.broadcasted_iota(jnp.int32, (q, q), 1)
    causal = row >= col
    lo = col < SSM_HEADDIM

    dtr = dt_ref[...] + dtb_ref[...]
    dt = jnp.maximum(dtr, 0.0) + jnp.log1p(jnp.exp(-jnp.abs(dtr)))
    a = dt * aneg_ref[...]
    cum = jnp.dot(causal.astype(F32), a, precision=lax.Precision.HIGHEST,
                  preferred_element_type=F32)
    cum_t = cum.T
    dt_t = dt.T
    to_end_t = jnp.exp(cum_t[:, q - 1:q] - cum_t) * dt_t

    def conv(off, width):
        acc = cb_ref[:, off:off + width]
        for k in range(SSM_CONV):
            lo_row = CONV_TAIL - (SSM_CONV - 1) + k
            acc = acc + cw_ref[k:k + 1, off:off + width] * xpad_ref[lo_row:lo_row + q, off:off + width]
        return _silu(acc)

    for g in range(SSM_GROUPS):
        gx = SSM_GROUP_W * g
        xg = conv(gx, SSM_GROUP_W)
        bg = conv(SSM_D_INNER + SSM_D_STATE * g, SSM_D_STATE)
        cg = conv(SSM_D_INNER + SSM_GN + SSM_D_STATE * g, SSM_D_STATE)
        bb = bg.astype(BF16)
        cb = cg.astype(BF16)
        cbm = lax.dot_general(cb, bb, (((1,), (1,)), ((), ())), preferred_element_type=F32)
        bg_t = bg.T
        st = st_ref[g]
        y_inter = jnp.dot(cb, st.astype(BF16), preferred_element_type=F32)

        y_pairs = []
        for p in range(SSM_HPG // 2):
            cols = slice(LANES * p, LANES * (p + 1))
            xp = xg[:, cols]
            x2 = jnp.concatenate([jnp.where(lo, xp, 0.0), jnp.where(lo, 0.0, xp)], axis=0).astype(BF16)
            wgts, bts, cis, decs = [], [], [], []
            for hh in (SSM_HPG * g + 2 * p, SSM_HPG * g + 2 * p + 1):
                ci = jnp.broadcast_to(cum[:, hh:hh + 1], (q, q))
                cj = cum_t[hh:hh + 1, :]
                decay = jnp.exp(jnp.where(causal, ci - cj, -jnp.inf))
                wgts.append((decay * cbm * dt_t[hh:hh + 1, :]).astype(BF16))
                bts.append((bg_t * to_end_t[hh:hh + 1, :]).astype(BF16))
                cis.append(ci)
                decs.append(jnp.exp(cum_t[hh:hh + 1, q - 1:q]))
            y_intra = jnp.dot(jnp.concatenate(wgts, axis=1), x2, preferred_element_type=F32)
            y_pairs.append(y_intra + y_inter[:, cols] * jnp.exp(jnp.where(lo, cis[0], cis[1])))
            d_st = jnp.dot(jnp.concatenate(bts, axis=1), x2, preferred_element_type=F32)
            dec = jnp.where(lo[0:1, :], decs[0], decs[1])
            st_ref[g, :, cols] = st[:, cols] * dec + d_st

        yg = jnp.concatenate(y_pairs, axis=1)
        yg = yg + dskip_ref[:, gx:gx + SSM_GROUP_W] * xg
        yg = yg * _silu(zx_ref[:, gx:gx + SSM_GROUP_W].astype(F32))
        y_ref[:, gx:gx + SSM_GROUP_W] = _rms_rows(yg, nw_ref[:, gx:gx + SSM_GROUP_W]).astype(y_ref.dtype)

    xpad_ref[0:CONV_TAIL, :] = xpad_ref[q:q + CONV_TAIL, :]


def _ssd(zx, dt, conv_w, conv_b, dt_bias, a_neg, d_skip, norm_w, batch, name):
    m = zx.shape[0]
    nc = m // batch // CHUNK
    row_blk = lambda b, c: (b * nc + c, 0)
    const = lambda b, c: (0, 0)
    return pl.pallas_call(
        _ssd_kernel,
        grid=(batch, nc),
        in_specs=[pl.BlockSpec((CHUNK, zx.shape[1]), row_blk),
                  pl.BlockSpec((CHUNK, LANES), row_blk),
                  pl.BlockSpec((SSM_CONV, SSM_CONV_DIM), const),
                  pl.BlockSpec((1, SSM_CONV_DIM), const),
                  pl.BlockSpec((1, LANES), const),
                  pl.BlockSpec((1, LANES), const),
                  pl.BlockSpec((1, SSM_D_INNER), const),
                  pl.BlockSpec((1, SSM_D_INNER), const)],
        out_specs=pl.BlockSpec((CHUNK, SSM_D_INNER), row_blk),
        out_shape=jax.ShapeDtypeStruct((m, SSM_D_INNER), BF16),
        scratch_shapes=[pltpu.VMEM((CONV_TAIL + CHUNK, SSM_CONV_DIM), F32),
                        pltpu.VMEM((SSM_GROUPS, SSM_D_STATE, SSM_GROUP_W), F32)],
        compiler_params=_params("parallel", "arbitrary"),
        name=name,
    )(zx, dt, conv_w, conv_b.reshape(1, -1), dt_bias.reshape(1, -1), a_neg.reshape(1, -1),
      d_skip.reshape(1, -1), norm_w.reshape(1, -1))


_RET_LOG_GAMMA = [math.log(1.0 - 2.0 ** (-5.0 - h)) for h in range(RET_HEADS)]


def _ret_kernel(proj_ref, cos_ref, sin_ref, nw_ref, o_ref, st_ref, dm_ref, qd_ref, kd_ref):
    q = CHUNK

    @pl.when(pl.program_id(1) == 0)
    def _():
        st_ref[...] = jnp.zeros_like(st_ref)
        row = lax.broadcasted_iota(jnp.int32, (q, q), 0)
        col = lax.broadcasted_iota(jnp.int32, (q, q), 1)
        diff = (row - col).astype(F32)
        rowf = row.astype(F32)
        for h in range(RET_HEADS):
            lg = _RET_LOG_GAMMA[h]
            dm_ref[h] = jnp.exp(jnp.where(diff >= 0, diff * lg, -jnp.inf))
            qd_ref[h] = jnp.exp((rowf + 1.0) * lg)
            kd_ref[h] = jnp.exp((q - 1.0 - rowf) * lg)

    cos = cos_ref[...]
    sin = sin_ref[...]
    half = RET_DK // 2

    def rotary(off):
        t1 = proj_ref[:, off:off + half].astype(F32)
        t2 = proj_ref[:, off + half:off + RET_DK].astype(F32)
        return jnp.concatenate([t1 * cos - t2 * sin, t1 * sin + t2 * cos], axis=1)

    for h in range(RET_HEADS):
        qr = rotary(RET_DK * h)
        kr = rotary(RET_QK_DIM + RET_DK * h) * (RET_DK ** -0.5)
        v = proj_ref[:, 2 * RET_QK_DIM + RET_DV * h:2 * RET_QK_DIM + RET_DV * (h + 1)]
        gate = proj_ref[:, 2 * RET_QK_DIM + RET_V_DIM + RET_DV * h:
                        2 * RET_QK_DIM + RET_V_DIM + RET_DV * (h + 1)].astype(F32)
        qb = qr.astype(BF16)
        att = lax.dot_general(qb, kr.astype(BF16), (((1,), (1,)), ((), ())),
                              preferred_element_type=F32) * dm_ref[h]
        st = st_ref[h]
        qd = qd_ref[h]
        o = jnp.dot(att.astype(BF16), v, preferred_element_type=F32)
        o = o + jnp.dot(qb, st.astype(BF16), preferred_element_type=F32) * jnp.concatenate([qd] * (RET_DV // q), axis=1)
        kd = kd_ref[h]
        kdec = (kr * jnp.concatenate([kd] * (RET_DK // q), axis=1)).astype(BF16)
        st_ref[h] = st * math.exp(q * _RET_LOG_GAMMA[h]) + lax.dot_general(
            kdec, v, (((0,), (0,)), ((), ())), preferred_element_type=F32)
        vs = slice(RET_DV * h, RET_DV * (h + 1))
        o_ref[:, vs] = (_silu(gate) * _rms_rows(o, nw_ref[:, vs])).astype(o_ref.dtype)


def _retention(proj, cos, sin, norm_w, batch, name):
    m = proj.shape[0]
    nc = m // batch // CHUNK
    row_blk = lambda b, c: (b * nc + c, 0)
    return pl.pallas_call(
        _ret_kernel,
        grid=(batch, nc),
        in_specs=[pl.BlockSpec((CHUNK, proj.shape[1]), row_blk),
                  pl.BlockSpec((CHUNK, RET_DK // 2), lambda b, c: (c, 0)),
                  pl.BlockSpec((CHUNK, RET_DK // 2), lambda b, c: (c, 0)),
                  pl.BlockSpec((1, RET_V_DIM), lambda b, c: (0, 0))],
        out_specs=pl.BlockSpec((CHUNK, RET_V_DIM), row_blk),
        out_shape=jax.ShapeDtypeStruct((m, RET_V_DIM), BF16),
        scratch_shapes=[pltpu.VMEM((RET_HEADS, RET_DK, RET_DV), F32),
                        pltpu.VMEM((RET_HEADS, CHUNK, CHUNK), F32),
                        pltpu.VMEM((RET_HEADS, CHUNK, CHUNK), F32),
                        pltpu.VMEM((RET_HEADS, CHUNK, CHUNK), F32)],
        compiler_params=_params("parallel", "arbitrary"),
        name=name,
    )(proj, cos, sin, norm_w.reshape(1, -1))


def _deinterleave_heads(w, heads, dim):
    k = w.shape[0]
    return w.reshape(k, heads, dim // 2, 2).transpose(0, 1, 3, 2).reshape(k, heads * dim)


def _ffn(x, nw, wg, wu, wd, tag):
    hidden = _ffn_in(x, nw, wg.astype(BF16), wu.astype(BF16), 1024, 512, "ffn_in" + tag)
    return _matmul_resid(hidden, wd.astype(BF16), x, 1024, 512, "ffn_out" + tag)


def kernel(x, norm_mix, ssm_w_in, ssm_conv_w, ssm_conv_b, ssm_dt_bias, ssm_a_log, ssm_d, ssm_norm, ssm_w_out, ret_w_in, ret_norm, ret_w_out, norm_ffn, ffn_w_gate, ffn_w_up, ffn_w_down, norm_final):
    batch, seq, d = x.shape
    m = batch * seq
    xf = x.reshape(m, d)

    w_in = ssm_w_in[0]
    n_zx = SSM_D_INNER + SSM_CONV_DIM
    w_zx = w_in[:, :n_zx].astype(BF16)
    w_dt = jnp.pad(w_in[:, n_zx:], ((0, 0), (0, LANES - SSM_HEADS))).astype(BF16)
    zx = _rms_matmul(xf, norm_mix[0], w_zx, BF16, 1024, 1024, "ssm_in_zx")
    dt = _rms_matmul(xf, norm_mix[0], w_dt, F32, 1024, LANES, "ssm_in_dt")
    pad_h = (0, LANES - SSM_HEADS)
    y = _ssd(zx, dt, ssm_conv_w[0], ssm_conv_b[0],
             jnp.pad(ssm_dt_bias[0].astype(F32), pad_h),
             jnp.pad(-jnp.exp(ssm_a_log[0].astype(F32)), pad_h),
             jnp.repeat(ssm_d[0].astype(F32), SSM_HEADDIM), ssm_norm[0], batch, "ssd")
    xf = _matmul_resid(y, ssm_w_out[0].astype(BF16), xf, 1024, 512, "ssm_out")
    xf = _ffn(xf, norm_ffn[0], ffn_w_gate[0], ffn_w_up[0], ffn_w_down[0], "0")

    w_in = ret_w_in[0]
    w_q = _deinterleave_heads(w_in[:, :RET_QK_DIM], RET_HEADS, RET_DK)
    w_k = _deinterleave_heads(w_in[:, RET_QK_DIM:2 * RET_QK_DIM], RET_HEADS, RET_DK)
    w_ret = jnp.concatenate([w_q, w_k, w_in[:, 2 * RET_QK_DIM:]], axis=1).astype(BF16)
    proj = _rms_matmul(xf, norm_mix[1], w_ret, BF16, 1024, 1024, "ret_in")
    freq = 1.0 / (ROPE_BASE ** jnp.linspace(0.0, 1.0, RET_DK // 2, dtype=F32))
    ang = jnp.arange(seq, dtype=F32)[:, None] * freq[None, :]
    o = _retention(proj, jnp.cos(ang), jnp.sin(ang), ret_norm[0], batch, "retention")
    xf = _matmul_resid(o, ret_w_out[0].astype(BF16), xf, 1024, 512, "ret_out")
    xf = _ffn(xf, norm_ffn[1], ffn_w_gate[1], ffn_w_up[1], ffn_w_down[1], "1")

    return _rms(xf, norm_final, 512, "final_norm").reshape(batch, seq, d)
```

```python
import functools
import math

import jax
import jax.numpy as jnp
from jax import lax
from jax.experimental import pallas as pl
from jax.experimental.pallas import tpu as pltpu

F32 = jnp.float32
BF16 = jnp.bfloat16

D_MODEL = 2048
NORM_EPS = 1e-6
CHUNK = 128

SSM_D_INNER = 4096
SSM_HEADDIM = 64
SSM_HEADS = 64
SSM_D_STATE = 128
SSM_GROUPS = 8
SSM_HPG = 8
SSM_CONV = 4
SSM_GN = SSM_GROUPS * SSM_D_STATE
SSM_CONV_DIM = SSM_D_INNER + 2 * SSM_GN
SSM_GROUP_W = SSM_HPG * SSM_HEADDIM
CONV_TAIL = 8

RET_HEADS = 8
RET_QK_DIM = 2048
RET_V_DIM = 4096
RET_DK = 256
RET_DV = 512
ROPE_BASE = 10000.0

LANES = 128
VMEM_LIMIT = 56 * 1024 * 1024


def _params(*sem):
    return pltpu.CompilerParams(dimension_semantics=sem, vmem_limit_bytes=VMEM_LIMIT)


def _silu(x):
    h = 0.5 * x
    return h + h * jnp.tanh(h)


def _rms_rows(x, w):
    ms = jnp.mean(x * x, axis=-1, keepdims=True)
    return x * lax.rsqrt(ms + NORM_EPS) * w


def _w_spec(k, tn, layer):
    return pl.BlockSpec((None, k, tn), lambda i, j: (layer, 0, j))


def _rms_matmul_kernel(x_ref, nw_ref, w_ref, o_ref, h_ref):
    @pl.when(pl.program_id(1) == 0)
    def _():
        h_ref[...] = _rms_rows(x_ref[...], nw_ref[...]).astype(BF16)

    o_ref[...] = jnp.dot(h_ref[...], w_ref[...], preferred_element_type=F32).astype(o_ref.dtype)


def _rms_matmul(x, nw, w, layer, n, out_dtype, tm, tn, name):
    m, k = x.shape
    return pl.pallas_call(
        _rms_matmul_kernel,
        grid=(m // tm, n // tn),
        in_specs=[pl.BlockSpec((tm, k), lambda i, j: (i, 0)),
                  pl.BlockSpec((1, k), lambda i, j: (0, 0)),
                  _w_spec(k, tn, layer)],
        out_specs=pl.BlockSpec((tm, tn), lambda i, j: (i, j)),
        out_shape=jax.ShapeDtypeStruct((m, n), out_dtype),
        scratch_shapes=[pltpu.VMEM((tm, k), BF16)],
        compiler_params=_params("parallel", "arbitrary"),
        name=name,
    )(x, nw.reshape(1, k), w)


def _ffn_in_kernel(x_ref, nw_ref, wg_ref, wu_ref, o_ref, h_ref):
    @pl.when(pl.program_id(1) == 0)
    def _():
        h_ref[...] = _rms_rows(x_ref[...], nw_ref[...]).astype(BF16)

    h = h_ref[...]
    g = jnp.dot(h, wg_ref[...], preferred_element_type=F32)
    u = jnp.dot(h, wu_ref[...], preferred_element_type=F32)
    o_ref[...] = (_silu(g) * u).astype(o_ref.dtype)


def _ffn_in(x, nw, wg, wu, layer, tm, tn, name):
    m, k = x.shape
    n = wg.shape[2]
    return pl.pallas_call(
        _ffn_in_kernel,
        grid=(m // tm, n // tn),
        in_specs=[pl.BlockSpec((tm, k), lambda i, j: (i, 0)),
                  pl.BlockSpec((1, k), lambda i, j: (0, 0)),
                  _w_spec(k, tn, layer),
                  _w_spec(k, tn, layer)],
        out_specs=pl.BlockSpec((tm, tn), lambda i, j: (i, j)),
        out_shape=jax.ShapeDtypeStruct((m, n), BF16),
        scratch_shapes=[pltpu.VMEM((tm, k), BF16)],
        compiler_params=_params("parallel", "arbitrary"),
        name=name,
    )(x, nw.reshape(1, k), wg, wu)


def _matmul_resid_kernel(a_ref, w_ref, r_ref, o_ref):
    o_ref[...] = r_ref[...] + jnp.dot(a_ref[...], w_ref[...], preferred_element_type=F32)


def _matmul_resid(a, w, layer, resid, tm, tn, name):
    m, k = a.shape
    n = w.shape[2]
    return pl.pallas_call(
        _matmul_resid_kernel,
        grid=(m // tm, n // tn),
        in_specs=[pl.BlockSpec((tm, k), lambda i, j: (i, 0)),
                  _w_spec(k, tn, layer),
                  pl.BlockSpec((tm, tn), lambda i, j: (i, j))],
        out_specs=pl.BlockSpec((tm, tn), lambda i, j: (i, j)),
        out_shape=jax.ShapeDtypeStruct((m, n), F32),
        compiler_params=_params("parallel", "parallel"),
        name=name,
    )(a, w, resid)


def _matmul_resid_norm_kernel(a_ref, w_ref, r_ref, nw_ref, o_ref, *, nj, tn):
    j = pl.program_id(1)
    val = r_ref[...] + jnp.dot(a_ref[...], w_ref[...], preferred_element_type=F32)
    for jj in range(nj):
        @pl.when(j == jj)
        def _():
            o_ref[:, jj * tn:(jj + 1) * tn] = val

    @pl.when(j == nj - 1)
    def _():
        o_ref[...] = _rms_rows(o_ref[...], nw_ref[...])


def _matmul_resid_norm(a, w, layer, resid, nw, tm, tn, name):
    m, k = a.shape
    n = w.shape[2]
    nj = n // tn
    return pl.pallas_call(
        functools.partial(_matmul_resid_norm_kernel, nj=nj, tn=tn),
        grid=(m // tm, nj),
        in_specs=[pl.BlockSpec((tm, k), lambda i, j: (i, 0)),
                  _w_spec(k, tn, layer),
                  pl.BlockSpec((tm, tn), lambda i, j: (i, j)),
                  pl.BlockSpec((1, n), lambda i, j: (0, 0))],
        out_specs=pl.BlockSpec((tm, n), lambda i, j: (i, 0)),
        out_shape=jax.ShapeDtypeStruct((m, n), F32),
        compiler_params=_params("parallel", "arbitrary"),
        name=name,
    )(a, w, resid, nw.reshape(1, n))


def _ssd_kernel(zx_ref, dt_ref, cw_ref, cb_ref, dtb_ref, aneg_ref, dskip_ref, nw_ref,
                y_ref, xpad_ref, st_ref):
    q = CHUNK

    @pl.when(pl.program_id(1) == 0)
    def _():
        xpad_ref[0:CONV_TAIL, :] = jnp.zeros((CONV_TAIL, SSM_CONV_DIM), F32)
        st_ref[...] = jnp.zeros_like(st_ref)

    xpad_ref[CONV_TAIL:CONV_TAIL + q, :] = zx_ref[:, SSM_D_INNER:].astype(F32)

    row = lax.broadcasted_iota(jnp.int32, (q, q), 0)
    col = lax.broadcasted_iota(jnp.int32, (q, q), 1)
    causal = row >= col
    lo = col < SSM_HEADDIM

    dtr = dt_ref[...] + dtb_ref[...]
    dt = jnp.maximum(dtr, 0.0) + jnp.log1p(jnp.exp(-jnp.abs(dtr)))
    a = dt * aneg_ref[...]
    cum = jnp.dot(causal.astype(F32), a, precision=lax.Precision.HIGHEST,
                  preferred_element_type=F32)
    cum_t = cum.T
    dt_t = dt.T
    to_end_t = jnp.exp(cum_t[:, q - 1:q] - cum_t) * dt_t

    def conv(off, width):
        acc = cb_ref[:, off:off + width]
        for k in range(SSM_CONV):
            lo_row = CONV_TAIL - (SSM_CONV - 1) + k
            acc = acc + cw_ref[k:k + 1, off:off + width] * xpad_ref[lo_row:lo_row + q, off:off + width]
        return _silu(acc)

    for g in range(SSM_GROUPS):
        gx = SSM_GROUP_W * g
        xg = conv(gx, SSM_GROUP_W)
        bg = conv(SSM_D_INNER + SSM_D_STATE * g, SSM_D_STATE)
        cg = conv(SSM_D_INNER + SSM_GN + SSM_D_STATE * g, SSM_D_STATE)
        bb = bg.astype(BF16)
        cb = cg.astype(BF16)
        cbm = lax.dot_general(cb, bb, (((1,), (1,)), ((), ())), preferred_element_type=F32)
        bg_t = bg.T
        st = st_ref[g]
        y_inter = jnp.dot(cb, st.astype(BF16), preferred_element_type=F32)

        y_pairs = []
        for p in range(SSM_HPG // 2):
            cols = slice(LANES * p, LANES * (p + 1))
            xp = xg[:, cols]
            x2 = jnp.concatenate([jnp.where(lo, xp, 0.0), jnp.where(lo, 0.0, xp)], axis=0).astype(BF16)
            wgts, bts, cis, decs = [], [], [], []
            for hh in (SSM_HPG * g + 2 * p, SSM_HPG * g + 2 * p + 1):
                ci = jnp.broadcast_to(cum[:, hh:hh + 1], (q, q))
                cj = cum_t[hh:hh + 1, :]
                decay = jnp.exp(jnp.where(causal, ci - cj, -jnp.inf))
                wgts.append((decay * cbm * dt_t[hh:hh + 1, :]).astype(BF16))
                bts.append((bg_t * to_end_t[hh:hh + 1, :]).astype(BF16))
                cis.append(ci)
                decs.append(jnp.exp(cum_t[hh:hh + 1, q - 1:q]))
            y_intra = jnp.dot(jnp.concatenate(wgts, axis=1), x2, preferred_element_type=F32)
            y_pairs.append(y_intra + y_inter[:, cols] * jnp.exp(jnp.where(lo, cis[0], cis[1])))
            d_st = jnp.dot(jnp.concatenate(bts, axis=1), x2, preferred_element_type=F32)
            dec = jnp.where(lo[0:1, :], decs[0], decs[1])
            st_ref[g, :, cols] = st[:, cols] * dec + d_st

        yg = jnp.concatenate(y_pairs, axis=1)
        yg = yg + dskip_ref[:, gx:gx + SSM_GROUP_W] * xg
        yg = yg * _silu(zx_ref[:, gx:gx + SSM_GROUP_W].astype(F32))
        y_ref[:, gx:gx + SSM_GROUP_W] = _rms_rows(yg, nw_ref[:, gx:gx + SSM_GROUP_W]).astype(y_ref.dtype)

    xpad_ref[0:CONV_TAIL, :] = xpad_ref[q:q + CONV_TAIL, :]


def _ssd(zx, dt, conv_w, conv_b, dt_bias, a_neg, d_skip, norm_w, batch, name):
    m = zx.shape[0]
    nc = m // batch // CHUNK
    row_blk = lambda b, c: (b * nc + c, 0)
    const = lambda b, c: (0, 0)
    return pl.pallas_call(
        _ssd_kernel,
        grid=(batch, nc),
        in_specs=[pl.BlockSpec((CHUNK, zx.shape[1]), row_blk),
                  pl.BlockSpec((CHUNK, LANES), row_blk),
                  pl.BlockSpec((SSM_CONV, SSM_CONV_DIM), const),
                  pl.BlockSpec((1, SSM_CONV_DIM), const),
                  pl.BlockSpec((1, LANES), const),
                  pl.BlockSpec((1, LANES), const),
                  pl.BlockSpec((1, SSM_D_INNER), const),
                  pl.BlockSpec((1, SSM_D_INNER), const)],
        out_specs=pl.BlockSpec((CHUNK, SSM_D_INNER), row_blk),
        out_shape=jax.ShapeDtypeStruct((m, SSM_D_INNER), BF16),
        scratch_shapes=[pltpu.VMEM((CONV_TAIL + CHUNK, SSM_CONV_DIM), F32),
                        pltpu.VMEM((SSM_GROUPS, SSM_D_STATE, SSM_GROUP_W), F32)],
        compiler_params=_params("parallel", "arbitrary"),
        name=name,
    )(zx, dt, conv_w, conv_b.reshape(1, -1), dt_bias.reshape(1, -1), a_neg.reshape(1, -1),
      d_skip.reshape(1, -1), norm_w.reshape(1, -1))


_RET_LOG_GAMMA = [math.log(1.0 - 2.0 ** (-5.0 - h)) for h in range(RET_HEADS)]


def _ret_kernel(proj_ref, cos_ref, sin_ref, perm_ref, nw_ref, o_ref, st_ref, dm_ref, qd_ref, kd_ref):
    q = CHUNK
    k_scale = RET_DK ** -0.5

    @pl.when(pl.program_id(1) == 0)
    def _():
        st_ref[...] = jnp.zeros_like(st_ref)
        row = lax.broadcasted_iota(jnp.int32, (q, q), 0)
        col = lax.broadcasted_iota(jnp.int32, (q, q), 1)
        diff = (row - col).astype(F32)
        rowf = row.astype(F32)
        for h in range(RET_HEADS):
            lg = _RET_LOG_GAMMA[h]
            dm_ref[h] = jnp.exp(jnp.where(diff >= 0, diff * lg, -jnp.inf)) * k_scale
            qd_ref[h] = jnp.exp((rowf + 1.0) * lg)
            kd_ref[h] = jnp.exp((q - 1.0 - rowf) * lg) * k_scale

    cos = cos_ref[...]
    sin = sin_ref[...]
    half = RET_DK // 2

    heads = range(RET_HEADS)

    def deinterleave(off):
        return jnp.dot(proj_ref[:, off:off + RET_DK], perm_ref[...], preferred_element_type=F32)

    def rotary(t):
        t1, t2 = t[:, :half], t[:, half:]
        return jnp.concatenate([t1 * cos - t2 * sin, t1 * sin + t2 * cos], axis=1)

    def v_of(h):
        return proj_ref[:, 2 * RET_QK_DIM + RET_DV * h:2 * RET_QK_DIM + RET_DV * (h + 1)]

    tq = [deinterleave(RET_DK * h) for h in heads]
    tk = [deinterleave(RET_QK_DIM + RET_DK * h) for h in heads]
    qb = [rotary(t).astype(BF16) for t in tq]
    kr = [rotary(t) for t in tk]
    att = [lax.dot_general(qb[h], kr[h].astype(BF16), (((1,), (1,)), ((), ())),
                           preferred_element_type=F32) for h in heads]
    o_st = [jnp.dot(qb[h], st_ref[h].astype(BF16), preferred_element_type=F32) for h in heads]
    for h in heads:
        o_att = jnp.dot((att[h] * dm_ref[h]).astype(BF16), v_of(h), preferred_element_type=F32)
        o = o_att + o_st[h] * jnp.concatenate([qd_ref[h]] * (RET_DV // q), axis=1)
        vs = slice(RET_DV * h, RET_DV * (h + 1))
        gate = proj_ref[:, 2 * RET_QK_DIM + RET_V_DIM + RET_DV * h:
                        2 * RET_QK_DIM + RET_V_DIM + RET_DV * (h + 1)].astype(F32)
        o_ref[:, vs] = (_silu(gate) * _rms_rows(o, nw_ref[:, vs])).astype(o_ref.dtype)
        kdec = (kr[h] * jnp.concatenate([kd_ref[h]] * (RET_DK // q), axis=1)).astype(BF16)
        st_ref[h] = st_ref[h] * math.exp(q * _RET_LOG_GAMMA[h]) + lax.dot_general(
            kdec, v_of(h), (((0,), (0,)), ((), ())), preferred_element_type=F32)


def _retention(proj, cos, sin, norm_w, batch, name):
    m = proj.shape[0]
    nc = m // batch // CHUNK
    row_blk = lambda b, c: (b * nc + c, 0)
    src = jnp.arange(RET_DK)[:, None]
    perm = (jnp.arange(RET_DK)[None, :] == (src % 2) * (RET_DK // 2) + src // 2).astype(BF16)
    return pl.pallas_call(
        _ret_kernel,
        grid=(batch, nc),
        in_specs=[pl.BlockSpec((CHUNK, proj.shape[1]), row_blk),
                  pl.BlockSpec((CHUNK, RET_DK // 2), lambda b, c: (c, 0)),
                  pl.BlockSpec((CHUNK, RET_DK // 2), lambda b, c: (c, 0)),
                  pl.BlockSpec((RET_DK, RET_DK), lambda b, c: (0, 0)),
                  pl.BlockSpec((1, RET_V_DIM), lambda b, c: (0, 0))],
        out_specs=pl.BlockSpec((CHUNK, RET_V_DIM), row_blk),
        out_shape=jax.ShapeDtypeStruct((m, RET_V_DIM), BF16),
        scratch_shapes=[pltpu.VMEM((RET_HEADS, RET_DK, RET_DV), F32),
                        pltpu.VMEM((RET_HEADS, CHUNK, CHUNK), F32),
                        pltpu.VMEM((RET_HEADS, CHUNK, CHUNK), F32),
                        pltpu.VMEM((RET_HEADS, CHUNK, CHUNK), F32)],
        compiler_params=_params("parallel", "arbitrary"),
        name=name,
    )(proj, cos, sin, perm, norm_w.reshape(1, -1))


def _ffn_hidden(x, nw, wg, wu, layer):
    return _ffn_in(x, nw, wg, wu, layer, 1024, 512, "ffn_in%d" % layer)


def kernel(x, norm_mix, ssm_w_in, ssm_conv_w, ssm_conv_b, ssm_dt_bias, ssm_a_log, ssm_d, ssm_norm, ssm_w_out, ret_w_in, ret_norm, ret_w_out, norm_ffn, ffn_w_gate, ffn_w_up, ffn_w_down, norm_final):
    batch, seq, d = x.shape
    m = batch * seq
    xf = x.reshape(m, d)
    wg, wu, wd = ffn_w_gate.astype(BF16), ffn_w_up.astype(BF16), ffn_w_down.astype(BF16)

    n_zx = SSM_D_INNER + SSM_CONV_DIM
    pad_h = (0, LANES - SSM_HEADS)
    w_dt = jnp.pad(ssm_w_in[:, :, n_zx:], ((0, 0), (0, 0), pad_h)).astype(BF16)
    zx = _rms_matmul(xf, norm_mix[0], ssm_w_in.astype(BF16), 0, n_zx, BF16, 1024, 1024, "ssm_in_zx")
    dt = _rms_matmul(xf, norm_mix[0], w_dt, 0, LANES, F32, 1024, LANES, "ssm_in_dt")
    y = _ssd(zx, dt, ssm_conv_w[0], ssm_conv_b[0],
             jnp.pad(ssm_dt_bias[0].astype(F32), pad_h),
             jnp.pad(-jnp.exp(ssm_a_log[0].astype(F32)), pad_h),
             jnp.repeat(ssm_d[0].astype(F32), SSM_HEADDIM), ssm_norm[0], batch, "ssd")
    xf = _matmul_resid(y, ssm_w_out.astype(BF16), 0, xf, 1024, 512, "ssm_out")
    hidden = _ffn_hidden(xf, norm_ffn[0], wg, wu, 0)
    xf = _matmul_resid(hidden, wd, 0, xf, 1024, 512, "ffn_out0")

    proj = _rms_matmul(xf, norm_mix[1], ret_w_in.astype(BF16), 0, ret_w_in.shape[2], BF16, 1024, 1024, "ret_in")
    freq = 1.0 / (ROPE_BASE ** jnp.linspace(0.0, 1.0, RET_DK // 2, dtype=F32))
    ang = jnp.arange(seq, dtype=F32)[:, None] * freq[None, :]
    o = _retention(proj, jnp.cos(ang), jnp.sin(ang), ret_norm[0], batch, "retention")
    xf = _matmul_resid(o, ret_w_out.astype(BF16), 0, xf, 1024, 512, "ret_out")
    hidden = _ffn_hidden(xf, norm_ffn[1], wg, wu, 1)
    out = _matmul_resid_norm(hidden, wd, 1, xf, norm_final, 512, 1024, "ffn_out1_norm")
    return out.reshape(batch, seq, d)
```

```python
import functools
import math

import jax
import jax.numpy as jnp
from jax import lax
from jax.experimental import pallas as pl
from jax.experimental.pallas import tpu as pltpu

F32 = jnp.float32
BF16 = jnp.bfloat16

D_MODEL = 2048
NORM_EPS = 1e-6
CHUNK = 128

SSM_D_INNER = 4096
SSM_HEADDIM = 64
SSM_HEADS = 64
SSM_D_STATE = 128
SSM_GROUPS = 8
SSM_HPG = 8
SSM_CONV = 4
SSM_GN = SSM_GROUPS * SSM_D_STATE
SSM_CONV_DIM = SSM_D_INNER + 2 * SSM_GN
SSM_GROUP_W = SSM_HPG * SSM_HEADDIM
CONV_TAIL = 16
CONV_AHEAD = 2
LOG2E = 1.0 / math.log(2.0)

RET_HEADS = 8
RET_QK_DIM = 2048
RET_V_DIM = 4096
RET_DK = 256
RET_DV = 512
ROPE_BASE = 10000.0
RET_CHUNKS_PER_STEP = 2

LANES = 128
VMEM_LIMIT = 56 * 1024 * 1024


def _params(*sem):
    return pltpu.CompilerParams(dimension_semantics=sem, vmem_limit_bytes=VMEM_LIMIT)


def _silu(x):
    h = 0.5 * x
    return h + h * jnp.tanh(h)


def _rms_rows(x, w):
    ms = jnp.mean(x * x, axis=-1, keepdims=True)
    return x * lax.rsqrt(ms + NORM_EPS) * w


def _w_spec(k, tn, layer):
    return pl.BlockSpec((None, k, tn), lambda i, j: (layer, 0, j))


def _rms_matmul_kernel(x_ref, nw_ref, w_ref, o_ref, h_ref):
    @pl.when(pl.program_id(1) == 0)
    def _():
        h_ref[...] = _rms_rows(x_ref[...], nw_ref[...]).astype(BF16)

    o_ref[...] = jnp.dot(h_ref[...], w_ref[...].astype(BF16), preferred_element_type=F32).astype(o_ref.dtype)


def _rms_matmul(x, nw, w, layer, n, out_dtype, tm, tn, name):
    m, k = x.shape
    return pl.pallas_call(
        _rms_matmul_kernel,
        grid=(m // tm, n // tn),
        in_specs=[pl.BlockSpec((tm, k), lambda i, j: (i, 0)),
                  pl.BlockSpec((1, k), lambda i, j: (0, 0)),
                  _w_spec(k, tn, layer)],
        out_specs=pl.BlockSpec((tm, tn), lambda i, j: (i, j)),
        out_shape=jax.ShapeDtypeStruct((m, n), out_dtype),
        scratch_shapes=[pltpu.VMEM((tm, k), BF16)],
        compiler_params=_params("parallel", "arbitrary"),
        name=name,
    )(x, nw.reshape(1, k), w)


def _rms_matmul2_kernel(x_ref, nw_ref, w_ref, w2_ref, o_ref, o2_ref, h_ref):
    @pl.when(pl.program_id(1) == 0)
    def _():
        h = _rms_rows(x_ref[...], nw_ref[...]).astype(BF16)
        h_ref[...] = h
        o2_ref[...] = jnp.dot(h, w2_ref[...], preferred_element_type=F32)

    o_ref[...] = jnp.dot(h_ref[...], w_ref[...].astype(BF16), preferred_element_type=F32).astype(o_ref.dtype)


def _rms_matmul2(x, nw, w, layer, n, w2, out_dtype, tm, tn, name):
    m, k = x.shape
    n2 = w2.shape[1]
    return pl.pallas_call(
        _rms_matmul2_kernel,
        grid=(m // tm, n // tn),
        in_specs=[pl.BlockSpec((tm, k), lambda i, j: (i, 0)),
                  pl.BlockSpec((1, k), lambda i, j: (0, 0)),
                  _w_spec(k, tn, layer),
                  pl.BlockSpec((k, n2), lambda i, j: (0, 0))],
        out_specs=[pl.BlockSpec((tm, tn), lambda i, j: (i, j)),
                   pl.BlockSpec((tm, n2), lambda i, j: (i, 0))],
        out_shape=[jax.ShapeDtypeStruct((m, n), out_dtype),
                   jax.ShapeDtypeStruct((m, n2), F32)],
        scratch_shapes=[pltpu.VMEM((tm, k), BF16)],
        compiler_params=_params("parallel", "arbitrary"),
        name=name,
    )(x, nw.reshape(1, k), w, w2)


def _ffn_in_kernel(x_ref, nw_ref, wg_ref, wu_ref, o_ref, h_ref):
    @pl.when(pl.program_id(1) == 0)
    def _():
        h_ref[...] = _rms_rows(x_ref[...], nw_ref[...]).astype(BF16)

    h = h_ref[...]
    g = jnp.dot(h, wg_ref[...].astype(BF16), preferred_element_type=F32)
    u = jnp.dot(h, wu_ref[...].astype(BF16), preferred_element_type=F32)
    o_ref[...] = (_silu(g) * u).astype(o_ref.dtype)


def _ffn_in(x, nw, wg, wu, layer, tm, tn, name):
    m, k = x.shape
    n = wg.shape[2]
    return pl.pallas_call(
        _ffn_in_kernel,
        grid=(m // tm, n // tn),
        in_specs=[pl.BlockSpec((tm, k), lambda i, j: (i, 0)),
                  pl.BlockSpec((1, k), lambda i, j: (0, 0)),
                  _w_spec(k, tn, layer),
                  _w_spec(k, tn, layer)],
        out_specs=pl.BlockSpec((tm, tn), lambda i, j: (i, j)),
        out_shape=jax.ShapeDtypeStruct((m, n), BF16),
        scratch_shapes=[pltpu.VMEM((tm, k), BF16)],
        compiler_params=_params("parallel", "arbitrary"),
        name=name,
    )(x, nw.reshape(1, k), wg, wu)


def _matmul_resid_kernel(a_ref, w_ref, r_ref, o_ref):
    o_ref[...] = r_ref[...] + jnp.dot(a_ref[...], w_ref[...], preferred_element_type=F32)


def _matmul_resid(a, w, layer, resid, tm, tn, name):
    m, k = a.shape
    n = w.shape[2]
    return pl.pallas_call(
        _matmul_resid_kernel,
        grid=(m // tm, n // tn),
        in_specs=[pl.BlockSpec((tm, k), lambda i, j: (i, 0)),
                  _w_spec(k, tn, layer),
                  pl.BlockSpec((tm, tn), lambda i, j: (i, j))],
        out_specs=pl.BlockSpec((tm, tn), lambda i, j: (i, j)),
        out_shape=jax.ShapeDtypeStruct((m, n), F32),
        compiler_params=_params("parallel", "parallel"),
        name=name,
    )(a, w, resid)


def _matmul_resid_norm_kernel(a_ref, w_ref, r_ref, nw_ref, o_ref, *, nj, tn):
    j = pl.program_id(1)
    val = r_ref[...] + jnp.dot(a_ref[...], w_ref[...], preferred_element_type=F32)
    for jj in range(nj):
        @pl.when(j == jj)
        def _():
            o_ref[:, jj * tn:(jj + 1) * tn] = val

    @pl.when(j == nj - 1)
    def _():
        o_ref[...] = _rms_rows(o_ref[...], nw_ref[...])


def _matmul_resid_norm(a, w, layer, resid, nw, tm, tn, name):
    m, k = a.shape
    n = w.shape[2]
    nj = n // tn
    return pl.pallas_call(
        functools.partial(_matmul_resid_norm_kernel, nj=nj, tn=tn),
        grid=(m // tm, nj),
        in_specs=[pl.BlockSpec((tm, k), lambda i, j: (i, 0)),
                  _w_spec(k, tn, layer),
                  pl.BlockSpec((tm, tn), lambda i, j: (i, j)),
                  pl.BlockSpec((1, n), lambda i, j: (0, 0))],
        out_specs=pl.BlockSpec((tm, n), lambda i, j: (i, 0)),
        out_shape=jax.ShapeDtypeStruct((m, n), F32),
        compiler_params=_params("parallel", "arbitrary"),
        name=name,
    )(a, w, resid, nw.reshape(1, n))


def _ssd_kernel(zx_ref, dt_ref, shift_ref, cw_ref, cb_ref, dtb_ref, aneg_ref, dskip_ref, nw_ref,
                y_ref, xpad_ref, st_ref):
    q = CHUNK

    @pl.when(pl.program_id(1) == 0)
    def _():
        xpad_ref[0:CONV_TAIL, :] = jnp.zeros((CONV_TAIL, SSM_CONV_DIM), BF16)
        st_ref[...] = jnp.zeros_like(st_ref)

    xpad_ref[CONV_TAIL:CONV_TAIL + q, :] = zx_ref[:, SSM_D_INNER:]

    row = lax.broadcasted_iota(jnp.int32, (q, q), 0)
    col = lax.broadcasted_iota(jnp.int32, (q, q), 1)
    causal = row >= col
    neg_mask = jnp.where(causal, 0.0, -jnp.inf)
    lo = col < SSM_HEADDIM

    dtr = dt_ref[...] + dtb_ref[...]
    dt = jnp.maximum(dtr, 0.0) + jnp.log1p(jnp.exp(-jnp.abs(dtr)))
    a = dt * aneg_ref[...]
    cum = jnp.dot(causal.astype(F32), a, precision=lax.Precision.HIGHEST,
                  preferred_element_type=F32) * LOG2E
    cum_t = cum.T
    dt_t = dt.T
    to_end_t = jnp.exp2(cum_t[:, q - 1:q] - cum_t) * dt_t

    def conv_offsets(g):
        return ((SSM_GROUP_W * g, SSM_GROUP_W),
                (SSM_D_INNER + SSM_D_STATE * g, SSM_D_STATE),
                (SSM_D_INNER + SSM_GN + SSM_D_STATE * g, SSM_D_STATE))

    def conv_shift(g):
        return [jnp.dot(shift_ref[...], xpad_ref[:, off:off + width], preferred_element_type=F32)
                for off, width in conv_offsets(g)]

    def conv_finish(g, shifted):
        outs = []
        for (off, width), sh in zip(conv_offsets(g), shifted):
            cur = zx_ref[:, SSM_D_INNER + off:SSM_D_INNER + off + width].astype(F32)
            acc = cb_ref[:, off:off + width] + cw_ref[SSM_CONV - 1:SSM_CONV, off:off + width] * cur
            for k in range(SSM_CONV - 1):
                acc = acc + cw_ref[k:k + 1, off:off + width] * sh[k * q:(k + 1) * q]
            outs.append(_silu(acc))
        return outs

    def prepare(g, shifted):
        xg, bg, cg = conv_finish(g, shifted)
        cb = cg.astype(BF16)
        cbm = lax.dot_general(cb, bg.astype(BF16), (((1,), (1,)), ((), ())),
                              preferred_element_type=F32)
        st = st_ref[g]
        y_inter = jnp.dot(cb, st.astype(BF16), preferred_element_type=F32)
        return xg, bg.T, cbm, st, y_inter

    def scan(g, xg, bg_t, cbm, st, y_inter):
        gx = SSM_GROUP_W * g
        y_pairs = []
        for p in range(SSM_HPG // 2):
            cols = slice(LANES * p, LANES * (p + 1))
            xp = xg[:, cols]
            x2 = jnp.concatenate([jnp.where(lo, xp, 0.0), jnp.where(lo, 0.0, xp)], axis=0).astype(BF16)
            wgts, bts, cis, decs = [], [], [], []
            for hh in (SSM_HPG * g + 2 * p, SSM_HPG * g + 2 * p + 1):
                ci = jnp.broadcast_to(cum[:, hh:hh + 1], (q, q))
                cj = cum_t[hh:hh + 1, :]
                decay = jnp.exp2(ci - cj + neg_mask)
                wgts.append((decay * cbm * dt_t[hh:hh + 1, :]).astype(BF16))
                bts.append((bg_t * to_end_t[hh:hh + 1, :]).astype(BF16))
                cis.append(ci)
                decs.append(jnp.exp2(cum_t[hh:hh + 1, q - 1:q]))
            y_intra = jnp.dot(jnp.concatenate(wgts, axis=1), x2, preferred_element_type=F32)
            y_pairs.append(y_intra + y_inter[:, cols] * jnp.exp2(jnp.where(lo, cis[0], cis[1])))
            d_st = jnp.dot(jnp.concatenate(bts, axis=1), x2, preferred_element_type=F32)
            dec = jnp.where(lo[0:1, :], decs[0], decs[1])
            st_ref[g, :, cols] = st[:, cols] * dec + d_st

        yg = jnp.concatenate(y_pairs, axis=1)
        yg = yg + dskip_ref[:, gx:gx + SSM_GROUP_W] * xg
        yg = yg * _silu(zx_ref[:, gx:gx + SSM_GROUP_W].astype(F32))
        y_ref[:, gx:gx + SSM_GROUP_W] = _rms_rows(yg, nw_ref[:, gx:gx + SSM_GROUP_W]).astype(y_ref.dtype)

    shifted = {g: conv_shift(g) for g in range(CONV_AHEAD)}
    ready = prepare(0, shifted.pop(0))
    for g in range(SSM_GROUPS):
        current = ready
        if g + CONV_AHEAD < SSM_GROUPS:
            shifted[g + CONV_AHEAD] = conv_shift(g + CONV_AHEAD)
        if g + 1 < SSM_GROUPS:
            ready = prepare(g + 1, shifted.pop(g + 1))
        scan(g, *current)

    xpad_ref[0:CONV_TAIL, :] = xpad_ref[q:q + CONV_TAIL, :]


def _ssd(zx, dt, conv_w, conv_b, dt_bias, a_neg, d_skip, norm_w, batch, name):
    m = zx.shape[0]
    nc = m // batch // CHUNK
    row_blk = lambda b, c: (b * nc + c, 0)
    const = lambda b, c: (0, 0)
    out_row = jnp.arange((SSM_CONV - 1) * CHUNK)[:, None]
    src_row = CONV_TAIL + out_row % CHUNK - (SSM_CONV - 1) + out_row // CHUNK
    shift = (jnp.arange(CONV_TAIL + CHUNK)[None, :] == src_row).astype(BF16)
    return pl.pallas_call(
        _ssd_kernel,
        grid=(batch, nc),
        in_specs=[pl.BlockSpec((CHUNK, zx.shape[1]), row_blk),
                  pl.BlockSpec((CHUNK, LANES), row_blk),
                  pl.BlockSpec(((SSM_CONV - 1) * CHUNK, CONV_TAIL + CHUNK), const),
                  pl.BlockSpec((SSM_CONV, SSM_CONV_DIM), const),
                  pl.BlockSpec((1, SSM_CONV_DIM), const),
                  pl.BlockSpec((1, LANES), const),
                  pl.BlockSpec((1, LANES), const),
                  pl.BlockSpec((1, SSM_D_INNER), const),
                  pl.BlockSpec((1, SSM_D_INNER), const)],
        out_specs=pl.BlockSpec((CHUNK, SSM_D_INNER), row_blk),
        out_shape=jax.ShapeDtypeStruct((m, SSM_D_INNER), BF16),
        scratch_shapes=[pltpu.VMEM((CONV_TAIL + CHUNK, SSM_CONV_DIM), BF16),
                        pltpu.VMEM((SSM_GROUPS, SSM_D_STATE, SSM_GROUP_W), F32)],
        compiler_params=_params("parallel", "arbitrary"),
        name=name,
    )(zx, dt, shift, conv_w, conv_b.reshape(1, -1), dt_bias.reshape(1, -1), a_neg.reshape(1, -1),
      d_skip.reshape(1, -1), norm_w.reshape(1, -1))


_RET_LOG_GAMMA = [math.log(1.0 - 2.0 ** (-5.0 - h)) for h in range(RET_HEADS)]


def _ret_kernel(proj_ref, cos_ref, sin_ref, perm_ref, nw_ref, o_ref, st_ref, dm_ref, qd_ref, kd_ref):
    q = CHUNK
    k_scale = RET_DK ** -0.5

    @pl.when(pl.program_id(1) == 0)
    def _():
        st_ref[...] = jnp.zeros_like(st_ref)
        row = lax.broadcasted_iota(jnp.int32, (q, q), 0)
        col = lax.broadcasted_iota(jnp.int32, (q, q), 1)
        diff = (row - col).astype(F32)
        rowf = row.astype(F32)
        for h in range(RET_HEADS):
            lg = _RET_LOG_GAMMA[h]
            dm_ref[h] = jnp.exp(jnp.where(diff >= 0, diff * lg, -jnp.inf)) * k_scale
            qd_ref[h] = jnp.exp((rowf + 1.0) * lg)
            kd_ref[h] = jnp.exp((q - 1.0 - rowf) * lg) * k_scale

    half = RET_DK // 2
    heads = range(RET_HEADS)

    def front(rows):
        cos = cos_ref[rows, :]
        sin = sin_ref[rows, :]

        def deinterleave(off):
            return jnp.dot(proj_ref[rows, off:off + RET_DK], perm_ref[...], preferred_element_type=F32)

        def rotary(t):
            t1, t2 = t[:, :half], t[:, half:]
            return jnp.concatenate([t1 * cos - t2 * sin, t1 * sin + t2 * cos], axis=1)

        tq = [deinterleave(RET_DK * h) for h in heads]
        tk = [deinterleave(RET_QK_DIM + RET_DK * h) for h in heads]
        qb = [rotary(t).astype(BF16) for t in tq]
        kr = [rotary(t) for t in tk]
        att = [lax.dot_general(qb[h], kr[h].astype(BF16), (((1,), (1,)), ((), ())),
                               preferred_element_type=F32) for h in heads]
        return qb, kr, att

    def back(rows, qb, kr, att):
        def v_of(h):
            return proj_ref[rows, 2 * RET_QK_DIM + RET_DV * h:2 * RET_QK_DIM + RET_DV * (h + 1)]

        o_st = [jnp.dot(qb[h], st_ref[h].astype(BF16), preferred_element_type=F32) for h in heads]
        for h in heads:
            o_att = jnp.dot((att[h] * dm_ref[h]).astype(BF16), v_of(h), preferred_element_type=F32)
            o = o_att + o_st[h] * jnp.concatenate([qd_ref[h]] * (RET_DV // q), axis=1)
            vs = slice(RET_DV * h, RET_DV * (h + 1))
            gate = proj_ref[rows, 2 * RET_QK_DIM + RET_V_DIM + RET_DV * h:
                            2 * RET_QK_DIM + RET_V_DIM + RET_DV * (h + 1)].astype(F32)
            o_ref[rows, vs] = (_silu(gate) * _rms_rows(o, nw_ref[:, vs])).astype(o_ref.dtype)
            kdec = (kr[h] * jnp.concatenate([kd_ref[h]] * (RET_DK // q), axis=1)).astype(BF16)
            st_ref[h] = st_ref[h] * math.exp(q * _RET_LOG_GAMMA[h]) + lax.dot_general(
                kdec, v_of(h), (((0,), (0,)), ((), ())), preferred_element_type=F32)

    chunk_rows = [slice(q * c, q * (c + 1)) for c in range(RET_CHUNKS_PER_STEP)]
    ready = front(chunk_rows[0])
    for c in range(RET_CHUNKS_PER_STEP):
        current = ready
        if c + 1 < RET_CHUNKS_PER_STEP:
            ready = front(chunk_rows[c + 1])
        back(chunk_rows[c], *current)


def _retention(proj, cos, sin, norm_w, batch, name):
    m = proj.shape[0]
    rows = CHUNK * RET_CHUNKS_PER_STEP
    nc = m // batch // rows
    row_blk = lambda b, c: (b * nc + c, 0)
    src = jnp.arange(RET_DK)[:, None]
    perm = (jnp.arange(RET_DK)[None, :] == (src % 2) * (RET_DK // 2) + src // 2).astype(BF16)
    return pl.pallas_call(
        _ret_kernel,
        grid=(batch, nc),
        in_specs=[pl.BlockSpec((rows, proj.shape[1]), row_blk),
                  pl.BlockSpec((rows, RET_DK // 2), lambda b, c: (c, 0)),
                  pl.BlockSpec((rows, RET_DK // 2), lambda b, c: (c, 0)),
                  pl.BlockSpec((RET_DK, RET_DK), lambda b, c: (0, 0)),
                  pl.BlockSpec((1, RET_V_DIM), lambda b, c: (0, 0))],
        out_specs=pl.BlockSpec((rows, RET_V_DIM), row_blk),
        out_shape=jax.ShapeDtypeStruct((m, RET_V_DIM), BF16),
        scratch_shapes=[pltpu.VMEM((RET_HEADS, RET_DK, RET_DV), F32),
                        pltpu.VMEM((RET_HEADS, CHUNK, CHUNK), F32),
                        pltpu.VMEM((RET_HEADS, CHUNK, CHUNK), F32),
                        pltpu.VMEM((RET_HEADS, CHUNK, CHUNK), F32)],
        compiler_params=_params("parallel", "arbitrary"),
        name=name,
    )(proj, cos, sin, perm, norm_w.reshape(1, -1))


def _ffn_hidden(x, nw, wg, wu, layer):
    return _ffn_in(x, nw, wg, wu, layer, 1024, 512, "ffn_in%d" % layer)


def kernel(x, norm_mix, ssm_w_in, ssm_conv_w, ssm_conv_b, ssm_dt_bias, ssm_a_log, ssm_d, ssm_norm, ssm_w_out, ret_w_in, ret_norm, ret_w_out, norm_ffn, ffn_w_gate, ffn_w_up, ffn_w_down, norm_final):
    batch, seq, d = x.shape
    m = batch * seq
    xf = x.reshape(m, d)
    wg, wu, wd = ffn_w_gate, ffn_w_up, ffn_w_down.astype(BF16)

    n_zx = SSM_D_INNER + SSM_CONV_DIM
    pad_h = (0, LANES - SSM_HEADS)
    w_dt = jnp.pad(ssm_w_in[0, :, n_zx:], ((0, 0), pad_h)).astype(BF16)
    zx, dt = _rms_matmul2(xf, norm_mix[0], ssm_w_in, 0, n_zx, w_dt, BF16, 1024, 1024, "ssm_in")
    y = _ssd(zx, dt, ssm_conv_w[0], ssm_conv_b[0],
             jnp.pad(ssm_dt_bias[0].astype(F32), pad_h),
             jnp.pad(-jnp.exp(ssm_a_log[0].astype(F32)), pad_h),
             jnp.repeat(ssm_d[0].astype(F32), SSM_HEADDIM), ssm_norm[0], batch, "ssd")
    xf = _matmul_resid(y, ssm_w_out.astype(BF16), 0, xf, 1024, 512, "ssm_out")
    hidden = _ffn_hidden(xf, norm_ffn[0], wg, wu, 0)
    xf = _matmul_resid(hidden, wd, 0, xf, 1024, 512, "ffn_out0")

    proj = _rms_matmul(xf, norm_mix[1], ret_w_in, 0, ret_w_in.shape[2], BF16, 1024, 1024, "ret_in")
    freq = 1.0 / (ROPE_BASE ** jnp.linspace(0.0, 1.0, RET_DK // 2, dtype=F32))
    ang = jnp.arange(seq, dtype=F32)[:, None] * freq[None, :]
    o = _retention(proj, jnp.cos(ang), jnp.sin(ang), ret_norm[0], batch, "retention")
    xf = _matmul_resid(o, ret_w_out.astype(BF16), 0, xf, 1024, 512, "ret_out")
    hidden = _ffn_hidden(xf, norm_ffn[1], wg, wu, 1)
    out = _matmul_resid_norm(hidden, wd, 1, xf, norm_final, 512, 1024, "ffn_out1_norm")
    return out.reshape(batch, seq, d)
```

```python
import functools
import math

import jax
import jax.numpy as jnp
from jax import lax
from jax.experimental import pallas as pl
from jax.experimental.pallas import tpu as pltpu

F32 = jnp.float32
BF16 = jnp.bfloat16

D_MODEL = 2048
NORM_EPS = 1e-6
CHUNK = 128

SSM_D_INNER = 4096
SSM_HEADDIM = 64
SSM_HEADS = 64
SSM_D_STATE = 128
SSM_GROUPS = 8
SSM_HPG = 8
SSM_CONV = 4
SSM_GN = SSM_GROUPS * SSM_D_STATE
SSM_CONV_DIM = SSM_D_INNER + 2 * SSM_GN
SSM_GROUP_W = SSM_HPG * SSM_HEADDIM
CONV_TAIL = 16
CONV_AHEAD = 2
LOG2E = 1.0 / math.log(2.0)

RET_HEADS = 8
RET_QK_DIM = 2048
RET_V_DIM = 4096
RET_DK = 256
RET_DV = 512
ROPE_BASE = 10000.0
RET_CHUNKS_PER_STEP = 2

LANES = 128
VMEM_LIMIT = 56 * 1024 * 1024


def _params(*sem):
    return pltpu.CompilerParams(dimension_semantics=sem, vmem_limit_bytes=VMEM_LIMIT)


def _silu(x):
    h = 0.5 * x
    return h + h * jnp.tanh(h)


def _rms_rows(x, w):
    ms = jnp.mean(x * x, axis=-1, keepdims=True)
    return x * lax.rsqrt(ms + NORM_EPS) * w


def _w_spec(k, tn, layer):
    return pl.BlockSpec((None, k, tn), lambda i, j: (layer, 0, j))


def _norm_matmul_kernel(x_ref, nw_ref, *refs, n_w, has_side, epilogue):
    w_refs = refs[:n_w]
    refs = refs[n_w:]
    if has_side:
        w2_ref, o_ref, o2_ref, h_ref, inv_ref = refs
    else:
        o_ref, h_ref, inv_ref = refs
    j = pl.program_id(1)

    def emit(h, inv):
        o_ref[...] = epilogue(*[jnp.dot(h, w[...], preferred_element_type=F32) * inv
                                for w in w_refs]).astype(o_ref.dtype)

    @pl.when(j == 0)
    def _():
        x = x_ref[...]
        inv = lax.rsqrt(jnp.mean(x * x, axis=-1, keepdims=True) + NORM_EPS)
        inv_ref[...] = jnp.broadcast_to(inv, inv_ref.shape)
        h = (x * nw_ref[...]).astype(BF16)
        h_ref[...] = h
        emit(h, inv)
        if has_side:
            o2_ref[...] = jnp.dot(h, w2_ref[...], preferred_element_type=F32) * inv

    @pl.when(j > 0)
    def _():
        emit(h_ref[...], inv_ref[:, 0:1])


def _norm_matmul(x, nw, ws, layer, n, out_dtype, tm, tn, name, epilogue=lambda y: y, w_side=None):
    m, k = x.shape
    in_specs = [pl.BlockSpec((tm, k), lambda i, j: (i, 0)),
                pl.BlockSpec((1, k), lambda i, j: (0, 0))] + [_w_spec(k, tn, layer) for _ in ws]
    out_specs = pl.BlockSpec((tm, tn), lambda i, j: (i, j))
    out_shape = jax.ShapeDtypeStruct((m, n), out_dtype)
    operands = [x, nw.reshape(1, k), *ws]
    if w_side is not None:
        n2 = w_side.shape[1]
        in_specs.append(pl.BlockSpec((k, n2), lambda i, j: (0, 0)))
        out_specs = [out_specs, pl.BlockSpec((tm, n2), lambda i, j: (i, 0))]
        out_shape = [out_shape, jax.ShapeDtypeStruct((m, n2), F32)]
        operands.append(w_side)
    return pl.pallas_call(
        functools.partial(_norm_matmul_kernel, n_w=len(ws), has_side=w_side is not None, epilogue=epilogue),
        grid=(m // tm, n // tn),
        in_specs=in_specs,
        out_specs=out_specs,
        out_shape=out_shape,
        scratch_shapes=[pltpu.VMEM((tm, k), BF16), pltpu.VMEM((tm, LANES), F32)],
        compiler_params=_params("parallel", "arbitrary"),
        name=name,
    )(*operands)


def _matmul_resid_kernel(a_ref, w_ref, r_ref, o_ref):
    o_ref[...] = r_ref[...] + jnp.dot(a_ref[...], w_ref[...], preferred_element_type=F32)


def _matmul_resid(a, w, layer, resid, tm, tn, name):
    m, k = a.shape
    n = w.shape[2]
    return pl.pallas_call(
        _matmul_resid_kernel,
        grid=(m // tm, n // tn),
        in_specs=[pl.BlockSpec((tm, k), lambda i, j: (i, 0)),
                  _w_spec(k, tn, layer),
                  pl.BlockSpec((tm, tn), lambda i, j: (i, j))],
        out_specs=pl.BlockSpec((tm, tn), lambda i, j: (i, j)),
        out_shape=jax.ShapeDtypeStruct((m, n), F32),
        compiler_params=_params("parallel", "parallel"),
        name=name,
    )(a, w, resid)


def _matmul_resid_norm_kernel(a_ref, w_ref, r_ref, nw_ref, o_ref, *, nj, tn):
    j = pl.program_id(1)
    val = r_ref[...] + jnp.dot(a_ref[...], w_ref[...], preferred_element_type=F32)
    for jj in range(nj):
        @pl.when(j == jj)
        def _():
            o_ref[:, jj * tn:(jj + 1) * tn] = val

    @pl.when(j == nj - 1)
    def _():
        o_ref[...] = _rms_rows(o_ref[...], nw_ref[...])


def _matmul_resid_norm(a, w, layer, resid, nw, tm, tn, name):
    m, k = a.shape
    n = w.shape[2]
    nj = n // tn
    return pl.pallas_call(
        functools.partial(_matmul_resid_norm_kernel, nj=nj, tn=tn),
        grid=(m // tm, nj),
        in_specs=[pl.BlockSpec((tm, k), lambda i, j: (i, 0)),
                  _w_spec(k, tn, layer),
                  pl.BlockSpec((tm, tn), lambda i, j: (i, j)),
                  pl.BlockSpec((1, n), lambda i, j: (0, 0))],
        out_specs=pl.BlockSpec((tm, n), lambda i, j: (i, 0)),
        out_shape=jax.ShapeDtypeStruct((m, n), F32),
        compiler_params=_params("parallel", "arbitrary"),
        name=name,
    )(a, w, resid, nw.reshape(1, n))


def _ssd_kernel(zx_ref, dt_ref, shift_ref, cw_ref, cb_ref, dtb_ref, aneg_ref, dskip_ref, nw_ref,
                y_ref, xpad_ref, st_ref):
    q = CHUNK

    @pl.when(pl.program_id(1) == 0)
    def _():
        xpad_ref[0:CONV_TAIL, :] = jnp.zeros((CONV_TAIL, SSM_CONV_DIM), BF16)
        st_ref[...] = jnp.zeros_like(st_ref)

    xpad_ref[CONV_TAIL:CONV_TAIL + q, :] = zx_ref[:, SSM_D_INNER:]

    row = lax.broadcasted_iota(jnp.int32, (q, q), 0)
    col = lax.broadcasted_iota(jnp.int32, (q, q), 1)
    causal = row >= col
    neg_mask = jnp.where(causal, 0.0, -jnp.inf)
    lo = col < SSM_HEADDIM

    dtr = dt_ref[...] + dtb_ref[...]
    dt = jnp.maximum(dtr, 0.0) + jnp.log1p(jnp.exp(-jnp.abs(dtr)))
    a = dt * aneg_ref[...]
    cum = jnp.dot(causal.astype(F32), a, precision=lax.Precision.HIGHEST,
                  preferred_element_type=F32) * LOG2E
    cum_t = cum.T
    dt_t = dt.T
    to_end_t = jnp.exp2(cum_t[:, q - 1:q] - cum_t) * dt_t

    def conv_offsets(g):
        return ((SSM_GROUP_W * g, SSM_GROUP_W),
                (SSM_D_INNER + SSM_D_STATE * g, SSM_D_STATE),
                (SSM_D_INNER + SSM_GN + SSM_D_STATE * g, SSM_D_STATE))

    def conv_shift(g):
        return [jnp.dot(shift_ref[...], xpad_ref[:, off:off + width], preferred_element_type=F32)
                for off, width in conv_offsets(g)]

    def conv_finish(g, shifted):
        outs = []
        for (off, width), sh in zip(conv_offsets(g), shifted):
            cur = zx_ref[:, SSM_D_INNER + off:SSM_D_INNER + off + width].astype(F32)
            acc = cb_ref[:, off:off + width] + cw_ref[SSM_CONV - 1:SSM_CONV, off:off + width] * cur
            for k in range(SSM_CONV - 1):
                acc = acc + cw_ref[k:k + 1, off:off + width] * sh[k * q:(k + 1) * q]
            outs.append(_silu(acc))
        return outs

    def prepare(g, shifted):
        xg, bg, cg = conv_finish(g, shifted)
        cb = cg.astype(BF16)
        cbm = lax.dot_general(cb, bg.astype(BF16), (((1,), (1,)), ((), ())),
                              preferred_element_type=F32)
        st = st_ref[g]
        y_inter = jnp.dot(cb, st.astype(BF16), preferred_element_type=F32)
        return xg, bg.T, cbm, st, y_inter

    def scan(g, xg, bg_t, cbm, st, y_inter):
        gx = SSM_GROUP_W * g
        y_pairs = []
        for p in range(SSM_HPG // 2):
            cols = slice(LANES * p, LANES * (p + 1))
            xp = xg[:, cols]
            x2 = jnp.concatenate([jnp.where(lo, xp, 0.0), jnp.where(lo, 0.0, xp)], axis=0).astype(BF16)
            wgts, bts, cis, decs = [], [], [], []
            for hh in (SSM_HPG * g + 2 * p, SSM_HPG * g + 2 * p + 1):
                ci = jnp.broadcast_to(cum[:, hh:hh + 1], (q, q))
                cj = cum_t[hh:hh + 1, :]
                decay = jnp.exp2(ci - cj + neg_mask)
                wgts.append((decay * cbm * dt_t[hh:hh + 1, :]).astype(BF16))
                bts.append((bg_t * to_end_t[hh:hh + 1, :]).astype(BF16))
                cis.append(ci)
                decs.append(jnp.exp2(cum_t[hh:hh + 1, q - 1:q]))
            y_intra = jnp.dot(jnp.concatenate(wgts, axis=1), x2, preferred_element_type=F32)
            y_pairs.append(y_intra + y_inter[:, cols] * jnp.exp2(jnp.where(lo, cis[0], cis[1])))
            d_st = jnp.dot(jnp.concatenate(bts, axis=1), x2, preferred_element_type=F32)
            dec = jnp.where(lo[0:1, :], decs[0], decs[1])
            st_ref[g, :, cols] = st[:, cols] * dec + d_st

        yg = jnp.concatenate(y_pairs, axis=1)
        yg = yg + dskip_ref[:, gx:gx + SSM_GROUP_W] * xg
        yg = yg * _silu(zx_ref[:, gx:gx + SSM_GROUP_W].astype(F32))
        y_ref[:, gx:gx + SSM_GROUP_W] = _rms_rows(yg, nw_ref[:, gx:gx + SSM_GROUP_W]).astype(y_ref.dtype)

    shifted = {g: conv_shift(g) for g in range(CONV_AHEAD)}
    ready = prepare(0, shifted.pop(0))
    for g in range(SSM_GROUPS):
        current = ready
        if g + CONV_AHEAD < SSM_GROUPS:
            shifted[g + CONV_AHEAD] = conv_shift(g + CONV_AHEAD)
        if g + 1 < SSM_GROUPS:
            ready = prepare(g + 1, shifted.pop(g + 1))
        scan(g, *current)

    xpad_ref[0:CONV_TAIL, :] = xpad_ref[q:q + CONV_TAIL, :]


def _ssd(zx, dt, conv_w, conv_b, dt_bias, a_neg, d_skip, norm_w, batch, name):
    m = zx.shape[0]
    nc = m // batch // CHUNK
    row_blk = lambda b, c: (b * nc + c, 0)
    const = lambda b, c: (0, 0)
    out_row = jnp.arange((SSM_CONV - 1) * CHUNK)[:, None]
    src_row = CONV_TAIL + out_row % CHUNK - (SSM_CONV - 1) + out_row // CHUNK
    shift = (jnp.arange(CONV_TAIL + CHUNK)[None, :] == src_row).astype(BF16)
    return pl.pallas_call(
        _ssd_kernel,
        grid=(batch, nc),
        in_specs=[pl.BlockSpec((CHUNK, zx.shape[1]), row_blk),
                  pl.BlockSpec((CHUNK, LANES), row_blk),
                  pl.BlockSpec(((SSM_CONV - 1) * CHUNK, CONV_TAIL + CHUNK), const),
                  pl.BlockSpec((SSM_CONV, SSM_CONV_DIM), const),
                  pl.BlockSpec((1, SSM_CONV_DIM), const),
                  pl.BlockSpec((1, LANES), const),
                  pl.BlockSpec((1, LANES), const),
                  pl.BlockSpec((1, SSM_D_INNER), const),
                  pl.BlockSpec((1, SSM_D_INNER), const)],
        out_specs=pl.BlockSpec((CHUNK, SSM_D_INNER), row_blk),
        out_shape=jax.ShapeDtypeStruct((m, SSM_D_INNER), BF16),
        scratch_shapes=[pltpu.VMEM((CONV_TAIL + CHUNK, SSM_CONV_DIM), BF16),
                        pltpu.VMEM((SSM_GROUPS, SSM_D_STATE, SSM_GROUP_W), F32)],
        compiler_params=_params("parallel", "arbitrary"),
        name=name,
    )(zx, dt, shift, conv_w, conv_b.reshape(1, -1), dt_bias.reshape(1, -1), a_neg.reshape(1, -1),
      d_skip.reshape(1, -1), norm_w.reshape(1, -1))


_RET_LOG_GAMMA = [math.log(1.0 - 2.0 ** (-5.0 - h)) for h in range(RET_HEADS)]


def _ret_kernel(proj_ref, cos_ref, sin_ref, perm_ref, nw_ref, o_ref, st_ref, dm_ref, qd_ref, kd_ref):
    q = CHUNK
    k_scale = RET_DK ** -0.5

    @pl.when(pl.program_id(1) == 0)
    def _():
        st_ref[...] = jnp.zeros_like(st_ref)
        row = lax.broadcasted_iota(jnp.int32, (q, q), 0)
        col = lax.broadcasted_iota(jnp.int32, (q, q), 1)
        diff = (row - col).astype(F32)
        rowf = row.astype(F32)
        for h in range(RET_HEADS):
            lg = _RET_LOG_GAMMA[h]
            dm_ref[h] = jnp.exp(jnp.where(diff >= 0, diff * lg, -jnp.inf)) * k_scale
            qd_ref[h] = jnp.exp((rowf + 1.0) * lg)
            kd_ref[h] = jnp.exp((q - 1.0 - rowf) * lg) * k_scale

    half = RET_DK // 2
    heads = range(RET_HEADS)

    def front(rows):
        cos = cos_ref[rows, :]
        sin = sin_ref[rows, :]

        def deinterleave(off):
            return jnp.dot(proj_ref[rows, off:off + RET_DK], perm_ref[...], preferred_element_type=F32)

        def rotary(t):
            t1, t2 = t[:, :half], t[:, half:]
            return jnp.concatenate([t1 * cos - t2 * sin, t1 * sin + t2 * cos], axis=1)

        tq = [deinterleave(RET_DK * h) for h in heads]
        tk = [deinterleave(RET_QK_DIM + RET_DK * h) for h in heads]
        qb = [rotary(t).astype(BF16) for t in tq]
        kr = [rotary(t) for t in tk]
        att = [lax.dot_general(qb[h], kr[h].astype(BF16), (((1,), (1,)), ((), ())),
                               preferred_element_type=F32) for h in heads]
        return qb, kr, att

    def back(rows, qb, kr, att):
        def v_of(h):
            return proj_ref[rows, 2 * RET_QK_DIM + RET_DV * h:2 * RET_QK_DIM + RET_DV * (h + 1)]

        o_st = [jnp.dot(qb[h], st_ref[h].astype(BF16), preferred_element_type=F32) for h in heads]
        for h in heads:
            o_att = jnp.dot((att[h] * dm_ref[h]).astype(BF16), v_of(h), preferred_element_type=F32)
            o = o_att + o_st[h] * jnp.concatenate([qd_ref[h]] * (RET_DV // q), axis=1)
            vs = slice(RET_DV * h, RET_DV * (h + 1))
            gate = proj_ref[rows, 2 * RET_QK_DIM + RET_V_DIM + RET_DV * h:
                            2 * RET_QK_DIM + RET_V_DIM + RET_DV * (h + 1)].astype(F32)
            o_ref[rows, vs] = (_silu(gate) * _rms_rows(o, nw_ref[:, vs])).astype(o_ref.dtype)
            kdec = (kr[h] * jnp.concatenate([kd_ref[h]] * (RET_DK // q), axis=1)).astype(BF16)
            st_ref[h] = st_ref[h] * math.exp(q * _RET_LOG_GAMMA[h]) + lax.dot_general(
                kdec, v_of(h), (((0,), (0,)), ((), ())), preferred_element_type=F32)

    chunk_rows = [slice(q * c, q * (c + 1)) for c in range(RET_CHUNKS_PER_STEP)]
    ready = front(chunk_rows[0])
    for c in range(RET_CHUNKS_PER_STEP):
        current = ready
        if c + 1 < RET_CHUNKS_PER_STEP:
            ready = front(chunk_rows[c + 1])
        back(chunk_rows[c], *current)


def _retention(proj, cos, sin, norm_w, batch, name):
    m = proj.shape[0]
    rows = CHUNK * RET_CHUNKS_PER_STEP
    nc = m // batch // rows
    row_blk = lambda b, c: (b * nc + c, 0)
    src = jnp.arange(RET_DK)[:, None]
    perm = (jnp.arange(RET_DK)[None, :] == (src % 2) * (RET_DK // 2) + src // 2).astype(BF16)
    return pl.pallas_call(
        _ret_kernel,
        grid=(batch, nc),
        in_specs=[pl.BlockSpec((rows, proj.shape[1]), row_blk),
                  pl.BlockSpec((rows, RET_DK // 2), lambda b, c: (c, 0)),
                  pl.BlockSpec((rows, RET_DK // 2), lambda b, c: (c, 0)),
                  pl.BlockSpec((RET_DK, RET_DK), lambda b, c: (0, 0)),
                  pl.BlockSpec((1, RET_V_DIM), lambda b, c: (0, 0))],
        out_specs=pl.BlockSpec((rows, RET_V_DIM), row_blk),
        out_shape=jax.ShapeDtypeStruct((m, RET_V_DIM), BF16),
        scratch_shapes=[pltpu.VMEM((RET_HEADS, RET_DK, RET_DV), F32),
                        pltpu.VMEM((RET_HEADS, CHUNK, CHUNK), F32),
                        pltpu.VMEM((RET_HEADS, CHUNK, CHUNK), F32),
                        pltpu.VMEM((RET_HEADS, CHUNK, CHUNK), F32)],
        compiler_params=_params("parallel", "arbitrary"),
        name=name,
    )(proj, cos, sin, perm, norm_w.reshape(1, -1))


def _ffn_hidden(x, nw, wg, wu, layer):
    return _norm_matmul(x, nw, [wg, wu], layer, wg.shape[2], BF16, 1024, 512, "ffn_in%d" % layer,
                        epilogue=lambda g, u: _silu(g) * u)


def kernel(x, norm_mix, ssm_w_in, ssm_conv_w, ssm_conv_b, ssm_dt_bias, ssm_a_log, ssm_d, ssm_norm, ssm_w_out, ret_w_in, ret_norm, ret_w_out, norm_ffn, ffn_w_gate, ffn_w_up, ffn_w_down, norm_final):
    batch, seq, d = x.shape
    m = batch * seq
    xf = x.reshape(m, d)
    wg, wu, wd = ffn_w_gate.astype(BF16), ffn_w_up.astype(BF16), ffn_w_down.astype(BF16)

    n_zx = SSM_D_INNER + SSM_CONV_DIM
    pad_h = (0, LANES - SSM_HEADS)
    w_dt = jnp.pad(ssm_w_in[0, :, n_zx:], ((0, 0), pad_h)).astype(BF16)
    zx, dt = _norm_matmul(xf, norm_mix[0], [ssm_w_in.astype(BF16)], 0, n_zx, BF16, 1024, 2048, "ssm_in",
                          w_side=w_dt)
    y = _ssd(zx, dt, ssm_conv_w[0], ssm_conv_b[0],
             jnp.pad(ssm_dt_bias[0].astype(F32), pad_h),
             jnp.pad(-jnp.exp(ssm_a_log[0].astype(F32)), pad_h),
             jnp.repeat(ssm_d[0].astype(F32), SSM_HEADDIM), ssm_norm[0], batch, "ssd")
    xf = _matmul_resid(y, ssm_w_out.astype(BF16), 0, xf, 1024, 512, "ssm_out")
    hidden = _ffn_hidden(xf, norm_ffn[0], wg, wu, 0)
    xf = _matmul_resid(hidden, wd, 0, xf, 1024, 512, "ffn_out0")

    proj = _norm_matmul(xf, norm_mix[1], [ret_w_in.astype(BF16)], 0, ret_w_in.shape[2], BF16, 1024, 2048, "ret_in")
    freq = 1.0 / (ROPE_BASE ** jnp.linspace(0.0, 1.0, RET_DK // 2, dtype=F32))
    ang = jnp.arange(seq, dtype=F32)[:, None] * freq[None, :]
    o = _retention(proj, jnp.cos(ang), jnp.sin(ang), ret_norm[0], batch, "retention")
    xf = _matmul_resid(o, ret_w_out.astype(BF16), 0, xf, 1024, 512, "ret_out")
    hidden = _ffn_hidden(xf, norm_ffn[1], wg, wu, 1)
    out = _matmul_resid_norm(hidden, wd, 1, xf, norm_final, 512, 1024, "ffn_out1_norm")
    return out.reshape(batch, seq, d)
```

```python
import functools
import math

import jax
import jax.numpy as jnp
from jax import lax
from jax.experimental import pallas as pl
from jax.experimental.pallas import tpu as pltpu

F32 = jnp.float32
BF16 = jnp.bfloat16

D_MODEL = 2048
NORM_EPS = 1e-6
CHUNK = 128

SSM_D_INNER = 4096
SSM_HEADDIM = 64
SSM_HEADS = 64
SSM_D_STATE = 128
SSM_GROUPS = 8
SSM_HPG = 8
SSM_CONV = 4
SSM_GN = SSM_GROUPS * SSM_D_STATE
SSM_CONV_DIM = SSM_D_INNER + 2 * SSM_GN
SSM_GROUP_W = SSM_HPG * SSM_HEADDIM
CONV_TAIL = 16
CONV_AHEAD = 2
LOG2E = 1.0 / math.log(2.0)

RET_HEADS = 8
RET_QK_DIM = 2048
RET_V_DIM = 4096
RET_DK = 256
RET_DV = 512
ROPE_BASE = 10000.0
RET_CHUNKS_PER_STEP = 2

LANES = 128
VMEM_LIMIT = 56 * 1024 * 1024


def _params(*sem):
    return pltpu.CompilerParams(dimension_semantics=sem, vmem_limit_bytes=VMEM_LIMIT)


def _silu(x):
    h = 0.5 * x
    return h + h * jnp.tanh(h)


def _rms_rows(x, w):
    ms = jnp.mean(x * x, axis=-1, keepdims=True)
    return x * lax.rsqrt(ms + NORM_EPS) * w


def _w_spec(k, tn, layer):
    return pl.BlockSpec((None, k, tn), lambda i, j: (layer, 0, j))


def _norm_matmul_kernel(x_ref, nw_ref, *refs, n_w, has_side, epilogue):
    w_refs = refs[:n_w]
    refs = refs[n_w:]
    if has_side:
        w2_ref, o_ref, o2_ref, h_ref, inv_ref = refs
    else:
        o_ref, h_ref, inv_ref = refs
    j = pl.program_id(1)

    def emit(h, inv):
        o_ref[...] = epilogue(*[jnp.dot(h, w[...], preferred_element_type=F32) * inv
                                for w in w_refs]).astype(o_ref.dtype)

    @pl.when(j == 0)
    def _():
        x = x_ref[...]
        inv = lax.rsqrt(jnp.mean(x * x, axis=-1, keepdims=True) + NORM_EPS)
        inv_ref[...] = jnp.broadcast_to(inv, inv_ref.shape)
        h = (x * nw_ref[...]).astype(BF16)
        h_ref[...] = h
        emit(h, inv)
        if has_side:
            o2_ref[...] = jnp.dot(h, w2_ref[...], preferred_element_type=F32) * inv

    @pl.when(j > 0)
    def _():
        emit(h_ref[...], inv_ref[:, 0:1])


def _norm_matmul(x, nw, ws, layer, n, out_dtype, tm, tn, name, epilogue=lambda y: y, w_side=None):
    m, k = x.shape
    in_specs = [pl.BlockSpec((tm, k), lambda i, j: (i, 0)),
                pl.BlockSpec((1, k), lambda i, j: (0, 0))] + [_w_spec(k, tn, layer) for _ in ws]
    out_specs = pl.BlockSpec((tm, tn), lambda i, j: (i, j))
    out_shape = jax.ShapeDtypeStruct((m, n), out_dtype)
    operands = [x, nw.reshape(1, k), *ws]
    if w_side is not None:
        n2 = w_side.shape[1]
        in_specs.append(pl.BlockSpec((k, n2), lambda i, j: (0, 0)))
        out_specs = [out_specs, pl.BlockSpec((tm, n2), lambda i, j: (i, 0))]
        out_shape = [out_shape, jax.ShapeDtypeStruct((m, n2), F32)]
        operands.append(w_side)
    return pl.pallas_call(
        functools.partial(_norm_matmul_kernel, n_w=len(ws), has_side=w_side is not None, epilogue=epilogue),
        grid=(m // tm, n // tn),
        in_specs=in_specs,
        out_specs=out_specs,
        out_shape=out_shape,
        scratch_shapes=[pltpu.VMEM((tm, k), BF16), pltpu.VMEM((tm, LANES), F32)],
        compiler_params=_params("parallel", "arbitrary"),
        name=name,
    )(*operands)


def _matmul_resid_kernel(a_ref, w_ref, r_ref, o_ref):
    o_ref[...] = r_ref[...] + jnp.dot(a_ref[...], w_ref[...], preferred_element_type=F32)


def _matmul_resid(a, w, layer, resid, tm, tn, name):
    m, k = a.shape
    n = w.shape[2]
    return pl.pallas_call(
        _matmul_resid_kernel,
        grid=(m // tm, n // tn),
        in_specs=[pl.BlockSpec((tm, k), lambda i, j: (i, 0)),
                  _w_spec(k, tn, layer),
                  pl.BlockSpec((tm, tn), lambda i, j: (i, j))],
        out_specs=pl.BlockSpec((tm, tn), lambda i, j: (i, j)),
        out_shape=jax.ShapeDtypeStruct((m, n), F32),
        compiler_params=_params("parallel", "parallel"),
        name=name,
    )(a, w, resid)


def _matmul_resid_norm_kernel(a_ref, w_ref, r_ref, nw_ref, o_ref, *, nj, tn):
    j = pl.program_id(1)
    val = r_ref[...] + jnp.dot(a_ref[...], w_ref[...], preferred_element_type=F32)
    for jj in range(nj):
        @pl.when(j == jj)
        def _():
            o_ref[:, jj * tn:(jj + 1) * tn] = val

    @pl.when(j == nj - 1)
    def _():
        o_ref[...] = _rms_rows(o_ref[...], nw_ref[...])


def _matmul_resid_norm(a, w, layer, resid, nw, tm, tn, name):
    m, k = a.shape
    n = w.shape[2]
    nj = n // tn
    return pl.pallas_call(
        functools.partial(_matmul_resid_norm_kernel, nj=nj, tn=tn),
        grid=(m // tm, nj),
        in_specs=[pl.BlockSpec((tm, k), lambda i, j: (i, 0)),
                  _w_spec(k, tn, layer),
                  pl.BlockSpec((tm, tn), lambda i, j: (i, j)),
                  pl.BlockSpec((1, n), lambda i, j: (0, 0))],
        out_specs=pl.BlockSpec((tm, n), lambda i, j: (i, 0)),
        out_shape=jax.ShapeDtypeStruct((m, n), F32),
        compiler_params=_params("parallel", "arbitrary"),
        name=name,
    )(a, w, resid, nw.reshape(1, n))


def _ssd_kernel(zx_ref, dt_ref, shift_ref, cw_ref, cb_ref, dtb_ref, aneg_ref, dskip_ref, nw_ref,
                y_ref, xpad_ref, st_ref):
    q = CHUNK

    @pl.when(pl.program_id(1) == 0)
    def _():
        xpad_ref[0:CONV_TAIL, :] = jnp.zeros((CONV_TAIL, SSM_CONV_DIM), BF16)
        st_ref[...] = jnp.zeros_like(st_ref)

    xpad_ref[CONV_TAIL:CONV_TAIL + q, :] = zx_ref[:, SSM_D_INNER:]

    row = lax.broadcasted_iota(jnp.int32, (q, q), 0)
    col = lax.broadcasted_iota(jnp.int32, (q, q), 1)
    causal = row >= col
    neg_mask = jnp.where(causal, 0.0, -jnp.inf)
    lo = col < SSM_HEADDIM

    dtr = dt_ref[...] + dtb_ref[...]
    dt = jnp.maximum(dtr, 0.0) + jnp.log1p(jnp.exp(-jnp.abs(dtr)))
    a = dt * aneg_ref[...]
    cum = jnp.dot(causal.astype(F32), a, precision=lax.Precision.HIGHEST,
                  preferred_element_type=F32) * LOG2E
    cum_t = cum.T
    dt_t = dt.T
    to_end_t = jnp.exp2(cum_t[:, q - 1:q] - cum_t) * dt_t

    def conv_offsets(g):
        return ((SSM_GROUP_W * g, SSM_GROUP_W),
                (SSM_D_INNER + SSM_D_STATE * g, SSM_D_STATE),
                (SSM_D_INNER + SSM_GN + SSM_D_STATE * g, SSM_D_STATE))

    def conv_shift(g):
        return [jnp.dot(shift_ref[...], xpad_ref[:, off:off + width], preferred_element_type=F32)
                for off, width in conv_offsets(g)]

    def conv_finish(g, shifted):
        outs = []
        for (off, width), sh in zip(conv_offsets(g), shifted):
            cur = zx_ref[:, SSM_D_INNER + off:SSM_D_INNER + off + width].astype(F32)
            acc = cb_ref[:, off:off + width] + cw_ref[SSM_CONV - 1:SSM_CONV, off:off + width] * cur
            for k in range(SSM_CONV - 1):
                acc = acc + cw_ref[k:k + 1, off:off + width] * sh[k * q:(k + 1) * q]
            outs.append(_silu(acc))
        return outs

    def prepare(g, shifted):
        xg, bg, cg = conv_finish(g, shifted)
        cb = cg.astype(BF16)
        cbm = lax.dot_general(cb, bg.astype(BF16), (((1,), (1,)), ((), ())),
                              preferred_element_type=F32)
        st = st_ref[g]
        y_inter = jnp.dot(cb, st.astype(BF16), preferred_element_type=F32)
        return xg, bg.T, cbm, st, y_inter

    def scan(g, xg, bg_t, cbm, st, y_inter):
        gx = SSM_GROUP_W * g
        y_pairs = []
        for p in range(SSM_HPG // 2):
            cols = slice(LANES * p, LANES * (p + 1))
            xp = xg[:, cols]
            x2 = jnp.concatenate([jnp.where(lo, xp, 0.0), jnp.where(lo, 0.0, xp)], axis=0).astype(BF16)
            wgts, bts, cis, decs = [], [], [], []
            for hh in (SSM_HPG * g + 2 * p, SSM_HPG * g + 2 * p + 1):
                ci = jnp.broadcast_to(cum[:, hh:hh + 1], (q, q))
                cj = cum_t[hh:hh + 1, :]
                decay = jnp.exp2(ci - cj + neg_mask)
                wgts.append((decay * cbm * dt_t[hh:hh + 1, :]).astype(BF16))
                bts.append((bg_t * to_end_t[hh:hh + 1, :]).astype(BF16))
                cis.append(ci)
                decs.append(jnp.exp2(cum_t[hh:hh + 1, q - 1:q]))
            y_intra = jnp.dot(jnp.concatenate(wgts, axis=1), x2, preferred_element_type=F32)
            y_pairs.append(y_intra + y_inter[:, cols] * jnp.exp2(jnp.where(lo, cis[0], cis[1])))
            d_st = jnp.dot(jnp.concatenate(bts, axis=1), x2, preferred_element_type=F32)
            dec = jnp.where(lo[0:1, :], decs[0], decs[1])
            st_ref[g, :, cols] = st[:, cols] * dec + d_st

        yg = jnp.concatenate(y_pairs, axis=1)
        yg = yg + dskip_ref[:, gx:gx + SSM_GROUP_W] * xg
        yg = yg * _silu(zx_ref[:, gx:gx + SSM_GROUP_W].astype(F32))
        y_ref[:, gx:gx + SSM_GROUP_W] = _rms_rows(yg, nw_ref[:, gx:gx + SSM_GROUP_W]).astype(y_ref.dtype)

    shifted = {g: conv_shift(g) for g in range(CONV_AHEAD)}
    ready = prepare(0, shifted.pop(0))
    for g in range(SSM_GROUPS):
        current = ready
        if g + CONV_AHEAD < SSM_GROUPS:
            shifted[g + CONV_AHEAD] = conv_shift(g + CONV_AHEAD)
        if g + 1 < SSM_GROUPS:
            ready = prepare(g + 1, shifted.pop(g + 1))
        scan(g, *current)

    xpad_ref[0:CONV_TAIL, :] = xpad_ref[q:q + CONV_TAIL, :]


def _ssd(zx, dt, conv_w, conv_b, dt_bias, a_neg, d_skip, norm_w, batch, name):
    m = zx.shape[0]
    nc = m // batch // CHUNK
    row_blk = lambda b, c: (b * nc + c, 0)
    const = lambda b, c: (0, 0)
    out_row = jnp.arange((SSM_CONV - 1) * CHUNK)[:, None]
    src_row = CONV_TAIL + out_row % CHUNK - (SSM_CONV - 1) + out_row // CHUNK
    shift = (jnp.arange(CONV_TAIL + CHUNK)[None, :] == src_row).astype(BF16)
    return pl.pallas_call(
        _ssd_kernel,
        grid=(batch, nc),
        in_specs=[pl.BlockSpec((CHUNK, zx.shape[1]), row_blk),
                  pl.BlockSpec((CHUNK, LANES), row_blk),
                  pl.BlockSpec(((SSM_CONV - 1) * CHUNK, CONV_TAIL + CHUNK), const),
                  pl.BlockSpec((SSM_CONV, SSM_CONV_DIM), const),
                  pl.BlockSpec((1, SSM_CONV_DIM), const),
                  pl.BlockSpec((1, LANES), const),
                  pl.BlockSpec((1, LANES), const),
                  pl.BlockSpec((1, SSM_D_INNER), const),
                  pl.BlockSpec((1, SSM_D_INNER), const)],
        out_specs=pl.BlockSpec((CHUNK, SSM_D_INNER), row_blk),
        out_shape=jax.ShapeDtypeStruct((m, SSM_D_INNER), BF16),
        scratch_shapes=[pltpu.VMEM((CONV_TAIL + CHUNK, SSM_CONV_DIM), BF16),
                        pltpu.VMEM((SSM_GROUPS, SSM_D_STATE, SSM_GROUP_W), F32)],
        compiler_params=_params("parallel", "arbitrary"),
        name=name,
    )(zx, dt, shift, conv_w, conv_b.reshape(1, -1), dt_bias.reshape(1, -1), a_neg.reshape(1, -1),
      d_skip.reshape(1, -1), norm_w.reshape(1, -1))


_RET_LOG_GAMMA = [math.log(1.0 - 2.0 ** (-5.0 - h)) for h in range(RET_HEADS)]


def _ret_kernel(proj_ref, cos_ref, sin_ref, perm_ref, nw_ref, o_ref, st_ref, dm_ref, qd_ref, kd_ref):
    q = CHUNK
    k_scale = RET_DK ** -0.5

    @pl.when(pl.program_id(1) == 0)
    def _():
        st_ref[...] = jnp.zeros_like(st_ref)
        row = lax.broadcasted_iota(jnp.int32, (q, q), 0)
        col = lax.broadcasted_iota(jnp.int32, (q, q), 1)
        diff = (row - col).astype(F32)
        rowf = row.astype(F32)
        for h in range(RET_HEADS):
            lg = _RET_LOG_GAMMA[h]
            dm_ref[h] = jnp.exp(jnp.where(diff >= 0, diff * lg, -jnp.inf)) * k_scale
            qd_ref[h] = jnp.exp((rowf + 1.0) * lg)
            kd_ref[h] = jnp.exp((q - 1.0 - rowf) * lg) * k_scale

    half = RET_DK // 2
    heads = range(RET_HEADS)

    def front(rows):
        cos = cos_ref[rows, :]
        sin = sin_ref[rows, :]

        def deinterleave(off):
            return jnp.dot(proj_ref[rows, off:off + RET_DK], perm_ref[...], preferred_element_type=F32)

        def rotary(t):
            t1, t2 = t[:, :half], t[:, half:]
            return jnp.concatenate([t1 * cos - t2 * sin, t1 * sin + t2 * cos], axis=1)

        tq = [deinterleave(RET_DK * h) for h in heads]
        tk = [deinterleave(RET_QK_DIM + RET_DK * h) for h in heads]
        qb = [rotary(t).astype(BF16) for t in tq]
        kr = [rotary(t) for t in tk]
        att = [lax.dot_general(qb[h], kr[h].astype(BF16), (((1,), (1,)), ((), ())),
                               preferred_element_type=F32) for h in heads]
        return qb, kr, att

    def back(rows, qb, kr, att):
        def v_of(h):
            return proj_ref[rows, 2 * RET_QK_DIM + RET_DV * h:2 * RET_QK_DIM + RET_DV * (h + 1)]

        o_st = [jnp.dot(qb[h], st_ref[h].astype(BF16), preferred_element_type=F32) for h in heads]
        for h in heads:
            o_att = jnp.dot((att[h] * dm_ref[h]).astype(BF16), v_of(h), preferred_element_type=F32)
            o = o_att + o_st[h] * jnp.concatenate([qd_ref[h]] * (RET_DV // q), axis=1)
            vs = slice(RET_DV * h, RET_DV * (h + 1))
            gate = proj_ref[rows, 2 * RET_QK_DIM + RET_V_DIM + RET_DV * h:
                            2 * RET_QK_DIM + RET_V_DIM + RET_DV * (h + 1)].astype(F32)
            o_ref[rows, vs] = (_silu(gate) * _rms_rows(o, nw_ref[:, vs])).astype(o_ref.dtype)
            kdec = (kr[h] * jnp.concatenate([kd_ref[h]] * (RET_DK // q), axis=1)).astype(BF16)
            st_ref[h] = st_ref[h] * math.exp(q * _RET_LOG_GAMMA[h]) + lax.dot_general(
                kdec, v_of(h), (((0,), (0,)), ((), ())), preferred_element_type=F32)

    chunk_rows = [slice(q * c, q * (c + 1)) for c in range(RET_CHUNKS_PER_STEP)]
    ready = front(chunk_rows[0])
    for c in range(RET_CHUNKS_PER_STEP):
        current = ready
        if c + 1 < RET_CHUNKS_PER_STEP:
            ready = front(chunk_rows[c + 1])
        back(chunk_rows[c], *current)


def _retention(proj, cos, sin, norm_w, batch, name):
    m = proj.shape[0]
    rows = CHUNK * RET_CHUNKS_PER_STEP
    nc = m // batch // rows
    row_blk = lambda b, c: (b * nc + c, 0)
    src = jnp.arange(RET_DK)[:, None]
    perm = (jnp.arange(RET_DK)[None, :] == (src % 2) * (RET_DK // 2) + src // 2).astype(BF16)
    return pl.pallas_call(
        _ret_kernel,
        grid=(batch, nc),
        in_specs=[pl.BlockSpec((rows, proj.shape[1]), row_blk),
                  pl.BlockSpec((rows, RET_DK // 2), lambda b, c: (c, 0)),
                  pl.BlockSpec((rows, RET_DK // 2), lambda b, c: (c, 0)),
                  pl.BlockSpec((RET_DK, RET_DK), lambda b, c: (0, 0)),
                  pl.BlockSpec((1, RET_V_DIM), lambda b, c: (0, 0))],
        out_specs=pl.BlockSpec((rows, RET_V_DIM), row_blk),
        out_shape=jax.ShapeDtypeStruct((m, RET_V_DIM), BF16),
        scratch_shapes=[pltpu.VMEM((RET_HEADS, RET_DK, RET_DV), F32),
                        pltpu.VMEM((RET_HEADS, CHUNK, CHUNK), F32),
                        pltpu.VMEM((RET_HEADS, CHUNK, CHUNK), F32),
                        pltpu.VMEM((RET_HEADS, CHUNK, CHUNK), F32)],
        compiler_params=_params("parallel", "arbitrary"),
        name=name,
    )(proj, cos, sin, perm, norm_w.reshape(1, -1))


def _ffn_hidden(x, nw, wg, wu, layer):
    return _norm_matmul(x, nw, [wg, wu], layer, wg.shape[2], BF16, 1024, 512, "ffn_in%d" % layer,
                        epilogue=lambda g, u: _silu(g) * u)


def kernel(x, norm_mix, ssm_w_in, ssm_conv_w, ssm_conv_b, ssm_dt_bias, ssm_a_log, ssm_d, ssm_norm, ssm_w_out, ret_w_in, ret_norm, ret_w_out, norm_ffn, ffn_w_gate, ffn_w_up, ffn_w_down, norm_final):
    batch, seq, d = x.shape
    m = batch * seq
    xf = x.reshape(m, d)
    wg, wu, wd = ffn_w_gate.astype(BF16), ffn_w_up.astype(BF16), ffn_w_down.astype(BF16)

    n_zx = SSM_D_INNER + SSM_CONV_DIM
    pad_h = (0, LANES - SSM_HEADS)
    w_dt = jnp.pad(ssm_w_in[0, :, n_zx:], ((0, 0), pad_h)).astype(BF16)
    zx, dt = _norm_matmul(xf, norm_mix[0], [ssm_w_in.astype(BF16)], 0, n_zx, BF16, 1024, 2048, "ssm_in",
                          w_side=w_dt)
    y = _ssd(zx, dt, ssm_conv_w[0], ssm_conv_b[0],
             jnp.pad(ssm_dt_bias[0].astype(F32), pad_h),
             jnp.pad(-jnp.exp(ssm_a_log[0].astype(F32)), pad_h),
             jnp.repeat(ssm_d[0].astype(F32), SSM_HEADDIM), ssm_norm[0], batch, "ssd")
    xf = _matmul_resid(y, ssm_w_out.astype(BF16), 0, xf, 2048, 256, "ssm_out")
    hidden = _ffn_hidden(xf, norm_ffn[0], wg, wu, 0)
    xf = _matmul_resid(hidden, wd, 0, xf, 1024, 512, "ffn_out0")

    proj = _norm_matmul(xf, norm_mix[1], [ret_w_in.astype(BF16)], 0, ret_w_in.shape[2], BF16, 1024, 2048, "ret_in")
    freq = 1.0 / (ROPE_BASE ** jnp.linspace(0.0, 1.0, RET_DK // 2, dtype=F32))
    ang = jnp.arange(seq, dtype=F32)[:, None] * freq[None, :]
    o = _retention(proj, jnp.cos(ang), jnp.sin(ang), ret_norm[0], batch, "retention")
    xf = _matmul_resid(o, ret_w_out.astype(BF16), 0, xf, 2048, 256, "ret_out")
    hidden = _ffn_hidden(xf, norm_ffn[1], wg, wu, 1)
    out = _matmul_resid_norm(hidden, wd, 1, xf, norm_final, 512, 1024, "ffn_out1_norm")
    return out.reshape(batch, seq, d)
```

```python
import functools
import math

import jax
import jax.numpy as jnp
from jax import lax
from jax.experimental import pallas as pl
from jax.experimental.pallas import tpu as pltpu

F32 = jnp.float32
BF16 = jnp.bfloat16

D_MODEL = 2048
NORM_EPS = 1e-6
CHUNK = 128

SSM_D_INNER = 4096
SSM_HEADDIM = 64
SSM_HEADS = 64
SSM_D_STATE = 128
SSM_GROUPS = 8
SSM_HPG = 8
SSM_CONV = 4
SSM_GN = SSM_GROUPS * SSM_D_STATE
SSM_CONV_DIM = SSM_D_INNER + 2 * SSM_GN
SSM_GROUP_W = SSM_HPG * SSM_HEADDIM
CONV_TAIL = 16
CONV_AHEAD = 2
LOG2E = 1.0 / math.log(2.0)

RET_HEADS = 8
RET_QK_DIM = 2048
RET_V_DIM = 4096
RET_DK = 256
RET_DV = 512
ROPE_BASE = 10000.0
RET_CHUNKS_PER_STEP = 2

LANES = 128
VMEM_LIMIT = 58 * 1024 * 1024


def _params(*sem):
    return pltpu.CompilerParams(dimension_semantics=sem, vmem_limit_bytes=VMEM_LIMIT)


def _silu(x):
    h = 0.5 * x
    return h + h * jnp.tanh(h)


def _rms_rows(x, w):
    ms = jnp.mean(x * x, axis=-1, keepdims=True)
    return x * lax.rsqrt(ms + NORM_EPS) * w


def _w_spec(k, tn, layer):
    return pl.BlockSpec((None, k, tn), lambda i, j: (layer, 0, j))


def _norm_matmul_kernel(x_ref, nw_ref, *refs, n_w, has_side, epilogue):
    w_refs = refs[:n_w]
    refs = refs[n_w:]
    if has_side:
        w2_ref, o_ref, o2_ref, h_ref, inv_ref = refs
    else:
        o_ref, h_ref, inv_ref = refs
    j = pl.program_id(1)

    def emit(h, inv):
        o_ref[...] = epilogue(*[jnp.dot(h, w[...], preferred_element_type=F32) * inv
                                for w in w_refs]).astype(o_ref.dtype)

    @pl.when(j == 0)
    def _():
        x = x_ref[...]
        inv = lax.rsqrt(jnp.mean(x * x, axis=-1, keepdims=True) + NORM_EPS)
        inv_ref[...] = jnp.broadcast_to(inv, inv_ref.shape)
        h = (x * nw_ref[...]).astype(BF16)
        h_ref[...] = h
        emit(h, inv)
        if has_side:
            o2_ref[...] = jnp.dot(h, w2_ref[...], preferred_element_type=F32) * inv

    @pl.when(j > 0)
    def _():
        emit(h_ref[...], inv_ref[:, 0:1])


def _norm_matmul(x, nw, ws, layer, n, out_dtype, tm, tn, name, epilogue=lambda y: y, w_side=None):
    m, k = x.shape
    in_specs = [pl.BlockSpec((tm, k), lambda i, j: (i, 0)),
                pl.BlockSpec((1, k), lambda i, j: (0, 0))] + [_w_spec(k, tn, layer) for _ in ws]
    out_specs = pl.BlockSpec((tm, tn), lambda i, j: (i, j))
    out_shape = jax.ShapeDtypeStruct((m, n), out_dtype)
    operands = [x, nw.reshape(1, k), *ws]
    if w_side is not None:
        n2 = w_side.shape[1]
        in_specs.append(pl.BlockSpec((k, n2), lambda i, j: (0, 0)))
        out_specs = [out_specs, pl.BlockSpec((tm, n2), lambda i, j: (i, 0))]
        out_shape = [out_shape, jax.ShapeDtypeStruct((m, n2), F32)]
        operands.append(w_side)
    return pl.pallas_call(
        functools.partial(_norm_matmul_kernel, n_w=len(ws), has_side=w_side is not None, epilogue=epilogue),
        grid=(m // tm, n // tn),
        in_specs=in_specs,
        out_specs=out_specs,
        out_shape=out_shape,
        scratch_shapes=[pltpu.VMEM((tm, k), BF16), pltpu.VMEM((tm, LANES), F32)],
        compiler_params=_params("parallel", "arbitrary"),
        name=name,
    )(*operands)


def _ffn_in_kernel(x_hbm, nw_ref, wg_ref, wu_ref, o_ref, x_buf, h_ref, inv_ref, x_sem):
    i = pl.program_id(0)
    j = pl.program_id(1)
    tm = x_buf.shape[0]

    def x_copy(tile):
        return pltpu.make_async_copy(x_hbm.at[pl.ds(tile * tm, tm), :], x_buf, x_sem)

    def emit(h, inv):
        g = jnp.dot(h, wg_ref[...], preferred_element_type=F32) * inv
        u = jnp.dot(h, wu_ref[...], preferred_element_type=F32) * inv
        o_ref[...] = (_silu(g) * u).astype(o_ref.dtype)

    @pl.when(jnp.logical_and(i == 0, j == 0))
    def _():
        x_copy(0).start()

    @pl.when(j == 0)
    def _():
        x_copy(i).wait()
        x = x_buf[...]
        inv = lax.rsqrt(jnp.mean(x * x, axis=-1, keepdims=True) + NORM_EPS)
        inv_ref[...] = jnp.broadcast_to(inv, inv_ref.shape)
        h = (x * nw_ref[...]).astype(BF16)
        h_ref[...] = h
        emit(h, inv)

    @pl.when(jnp.logical_and(j == 1, i + 1 < pl.num_programs(0)))
    def _():
        x_copy(i + 1).start()

    @pl.when(j > 0)
    def _():
        emit(h_ref[...], inv_ref[:, 0:1])


def _ffn_in(x, nw, wg, wu, layer, tm, tn, name):
    m, k = x.shape
    n = wg.shape[2]
    assert n // tn >= 2
    return pl.pallas_call(
        _ffn_in_kernel,
        grid=(m // tm, n // tn),
        in_specs=[pl.BlockSpec(memory_space=pl.ANY),
                  pl.BlockSpec((1, k), lambda i, j: (0, 0)),
                  _w_spec(k, tn, layer),
                  _w_spec(k, tn, layer)],
        out_specs=pl.BlockSpec((tm, tn), lambda i, j: (i, j)),
        out_shape=jax.ShapeDtypeStruct((m, n), BF16),
        scratch_shapes=[pltpu.VMEM((tm, k), F32), pltpu.VMEM((tm, k), BF16), pltpu.VMEM((tm, LANES), F32),
                        pltpu.SemaphoreType.DMA(())],
        compiler_params=_params("arbitrary", "arbitrary"),
        name=name,
    )(x, nw.reshape(1, k), wg, wu)


def _matmul_resid_kernel(a_ref, w_ref, r_ref, o_ref):
    o_ref[...] = r_ref[...] + jnp.dot(a_ref[...], w_ref[...], preferred_element_type=F32)


def _matmul_resid(a, w, layer, resid, tm, tn, name):
    m, k = a.shape
    n = w.shape[2]
    return pl.pallas_call(
        _matmul_resid_kernel,
        grid=(m // tm, n // tn),
        in_specs=[pl.BlockSpec((tm, k), lambda i, j: (i, 0)),
                  _w_spec(k, tn, layer),
                  pl.BlockSpec((tm, tn), lambda i, j: (i, j))],
        out_specs=pl.BlockSpec((tm, tn), lambda i, j: (i, j)),
        out_shape=jax.ShapeDtypeStruct((m, n), F32),
        compiler_params=_params("parallel", "parallel"),
        name=name,
    )(a, w, resid)


def _matmul_resid_norm_kernel(a_ref, w_ref, r_ref, nw_ref, o_ref, *, nj, tn):
    j = pl.program_id(1)
    val = r_ref[...] + jnp.dot(a_ref[...], w_ref[...], preferred_element_type=F32)
    for jj in range(nj):
        @pl.when(j == jj)
        def _():
            o_ref[:, jj * tn:(jj + 1) * tn] = val

    @pl.when(j == nj - 1)
    def _():
        o_ref[...] = _rms_rows(o_ref[...], nw_ref[...])


def _matmul_resid_norm(a, w, layer, resid, nw, tm, tn, name):
    m, k = a.shape
    n = w.shape[2]
    nj = n // tn
    return pl.pallas_call(
        functools.partial(_matmul_resid_norm_kernel, nj=nj, tn=tn),
        grid=(m // tm, nj),
        in_specs=[pl.BlockSpec((tm, k), lambda i, j: (i, 0)),
                  _w_spec(k, tn, layer),
                  pl.BlockSpec((tm, tn), lambda i, j: (i, j)),
                  pl.BlockSpec((1, n), lambda i, j: (0, 0))],
        out_specs=pl.BlockSpec((tm, n), lambda i, j: (i, 0)),
        out_shape=jax.ShapeDtypeStruct((m, n), F32),
        compiler_params=_params("parallel", "arbitrary"),
        name=name,
    )(a, w, resid, nw.reshape(1, n))


def _ssd_kernel(zx_ref, dt_ref, shift_ref, cw_ref, cb_ref, dtb_ref, aneg_ref, dskip_ref, nw_ref,
                y_ref, xpad_ref, st_ref):
    q = CHUNK

    @pl.when(pl.program_id(1) == 0)
    def _():
        xpad_ref[0:CONV_TAIL, :] = jnp.zeros((CONV_TAIL, SSM_CONV_DIM), BF16)
        st_ref[...] = jnp.zeros_like(st_ref)

    xpad_ref[CONV_TAIL:CONV_TAIL + q, :] = zx_ref[:, SSM_D_INNER:]

    row = lax.broadcasted_iota(jnp.int32, (q, q), 0)
    col = lax.broadcasted_iota(jnp.int32, (q, q), 1)
    causal = row >= col
    neg_mask = jnp.where(causal, 0.0, -jnp.inf)
    lo = col < SSM_HEADDIM

    dtr = dt_ref[...] + dtb_ref[...]
    dt = jnp.maximum(dtr, 0.0) + jnp.log1p(jnp.exp(-jnp.abs(dtr)))
    a = dt * aneg_ref[...]
    cum = jnp.dot(causal.astype(F32), a, precision=lax.Precision.HIGHEST,
                  preferred_element_type=F32) * LOG2E
    cum_t = cum.T
    dt_t = dt.T
    to_end_t = jnp.exp2(cum_t[:, q - 1:q] - cum_t) * dt_t

    def conv_offsets(g):
        return ((SSM_GROUP_W * g, SSM_GROUP_W),
                (SSM_D_INNER + SSM_D_STATE * g, SSM_D_STATE),
                (SSM_D_INNER + SSM_GN + SSM_D_STATE * g, SSM_D_STATE))

    def conv_shift(g):
        return [jnp.dot(shift_ref[...], xpad_ref[:, off:off + width], preferred_element_type=F32)
                for off, width in conv_offsets(g)]

    def conv_finish(g, shifted):
        outs = []
        for (off, width), sh in zip(conv_offsets(g), shifted):
            cur = zx_ref[:, SSM_D_INNER + off:SSM_D_INNER + off + width].astype(F32)
            acc = cb_ref[:, off:off + width] + cw_ref[SSM_CONV - 1:SSM_CONV, off:off + width] * cur
            for k in range(SSM_CONV - 1):
                acc = acc + cw_ref[k:k + 1, off:off + width] * sh[k * q:(k + 1) * q]
            outs.append(_silu(acc))
        return outs

    def prepare(g, shifted):
        xg, bg, cg = conv_finish(g, shifted)
        cb = cg.astype(BF16)
        cbm = lax.dot_general(cb, bg.astype(BF16), (((1,), (1,)), ((), ())),
                              preferred_element_type=F32)
        st = st_ref[g]
        y_inter = jnp.dot(cb, st.astype(BF16), preferred_element_type=F32)
        return xg, bg.T, cbm, st, y_inter

    def scan(g, xg, bg_t, cbm, st, y_inter):
        gx = SSM_GROUP_W * g
        y_pairs = []
        for p in range(SSM_HPG // 2):
            cols = slice(LANES * p, LANES * (p + 1))
            xp = xg[:, cols]
            x2 = jnp.concatenate([jnp.where(lo, xp, 0.0), jnp.where(lo, 0.0, xp)], axis=0).astype(BF16)
            wgts, bts, cis, decs = [], [], [], []
            for hh in (SSM_HPG * g + 2 * p, SSM_HPG * g + 2 * p + 1):
                ci = jnp.broadcast_to(cum[:, hh:hh + 1], (q, q))
                cj = cum_t[hh:hh + 1, :]
                decay = jnp.exp2(ci - cj + neg_mask)
                wgts.append((decay * cbm * dt_t[hh:hh + 1, :]).astype(BF16))
                bts.append((bg_t * to_end_t[hh:hh + 1, :]).astype(BF16))
                cis.append(ci)
                decs.append(jnp.exp2(cum_t[hh:hh + 1, q - 1:q]))
            y_intra = jnp.dot(jnp.concatenate(wgts, axis=1), x2, preferred_element_type=F32)
            y_pairs.append(y_intra + y_inter[:, cols] * jnp.exp2(jnp.where(lo, cis[0], cis[1])))
            d_st = jnp.dot(jnp.concatenate(bts, axis=1), x2, preferred_element_type=F32)
            dec = jnp.where(lo[0:1, :], decs[0], decs[1])
            st_ref[g, :, cols] = st[:, cols] * dec + d_st

        yg = jnp.concatenate(y_pairs, axis=1)
        yg = yg + dskip_ref[:, gx:gx + SSM_GROUP_W] * xg
        yg = yg * _silu(zx_ref[:, gx:gx + SSM_GROUP_W].astype(F32))
        y_ref[:, gx:gx + SSM_GROUP_W] = _rms_rows(yg, nw_ref[:, gx:gx + SSM_GROUP_W]).astype(y_ref.dtype)

    shifted = {g: conv_shift(g) for g in range(CONV_AHEAD)}
    ready = prepare(0, shifted.pop(0))
    for g in range(SSM_GROUPS):
        current = ready
        if g + CONV_AHEAD < SSM_GROUPS:
            shifted[g + CONV_AHEAD] = conv_shift(g + CONV_AHEAD)
        if g + 1 < SSM_GROUPS:
            ready = prepare(g + 1, shifted.pop(g + 1))
        scan(g, *current)

    xpad_ref[0:CONV_TAIL, :] = xpad_ref[q:q + CONV_TAIL, :]


def _ssd(zx, dt, conv_w, conv_b, dt_bias, a_neg, d_skip, norm_w, batch, name):
    m = zx.shape[0]
    nc = m // batch // CHUNK
    row_blk = lambda b, c: (b * nc + c, 0)
    const = lambda b, c: (0, 0)
    out_row = jnp.arange((SSM_CONV - 1) * CHUNK)[:, None]
    src_row = CONV_TAIL + out_row % CHUNK - (SSM_CONV - 1) + out_row // CHUNK
    shift = (jnp.arange(CONV_TAIL + CHUNK)[None, :] == src_row).astype(BF16)
    return pl.pallas_call(
        _ssd_kernel,
        grid=(batch, nc),
        in_specs=[pl.BlockSpec((CHUNK, zx.shape[1]), row_blk),
                  pl.BlockSpec((CHUNK, LANES), row_blk),
                  pl.BlockSpec(((SSM_CONV - 1) * CHUNK, CONV_TAIL + CHUNK), const),
                  pl.BlockSpec((SSM_CONV, SSM_CONV_DIM), const),
                  pl.BlockSpec((1, SSM_CONV_DIM), const),
                  pl.BlockSpec((1, LANES), const),
                  pl.BlockSpec((1, LANES), const),
                  pl.BlockSpec((1, SSM_D_INNER), const),
                  pl.BlockSpec((1, SSM_D_INNER), const)],
        out_specs=pl.BlockSpec((CHUNK, SSM_D_INNER), row_blk),
        out_shape=jax.ShapeDtypeStruct((m, SSM_D_INNER), BF16),
        scratch_shapes=[pltpu.VMEM((CONV_TAIL + CHUNK, SSM_CONV_DIM), BF16),
                        pltpu.VMEM((SSM_GROUPS, SSM_D_STATE, SSM_GROUP_W), F32)],
        compiler_params=_params("parallel", "arbitrary"),
        name=name,
    )(zx, dt, shift, conv_w, conv_b.reshape(1, -1), dt_bias.reshape(1, -1), a_neg.reshape(1, -1),
      d_skip.reshape(1, -1), norm_w.reshape(1, -1))


_RET_LOG_GAMMA = [math.log(1.0 - 2.0 ** (-5.0 - h)) for h in range(RET_HEADS)]


def _ret_kernel(proj_ref, cos_ref, sin_ref, perm_ref, nw_ref, o_ref, st_ref, dm_ref, qd_ref, kd_ref):
    q = CHUNK
    k_scale = RET_DK ** -0.5

    @pl.when(pl.program_id(1) == 0)
    def _():
        st_ref[...] = jnp.zeros_like(st_ref)
        row = lax.broadcasted_iota(jnp.int32, (q, q), 0)
        col = lax.broadcasted_iota(jnp.int32, (q, q), 1)
        diff = (row - col).astype(F32)
        rowf = row.astype(F32)
        for h in range(RET_HEADS):
            lg = _RET_LOG_GAMMA[h]
            dm_ref[h] = jnp.exp(jnp.where(diff >= 0, diff * lg, -jnp.inf)) * k_scale
            qd_ref[h] = jnp.exp((rowf + 1.0) * lg)
            kd_ref[h] = jnp.exp((q - 1.0 - rowf) * lg) * k_scale

    half = RET_DK // 2
    heads = range(RET_HEADS)

    def front(rows):
        cos = cos_ref[rows, :]
        sin = sin_ref[rows, :]

        def deinterleave(off):
            return jnp.dot(proj_ref[rows, off:off + RET_DK], perm_ref[...], preferred_element_type=F32)

        def rotary(t):
            t1, t2 = t[:, :half], t[:, half:]
            return jnp.concatenate([t1 * cos - t2 * sin, t1 * sin + t2 * cos], axis=1)

        tq = [deinterleave(RET_DK * h) for h in heads]
        tk = [deinterleave(RET_QK_DIM + RET_DK * h) for h in heads]
        qb = [rotary(t).astype(BF16) for t in tq]
        kr = [rotary(t) for t in tk]
        att = [lax.dot_general(qb[h], kr[h].astype(BF16), (((1,), (1,)), ((), ())),
                               preferred_element_type=F32) for h in heads]
        return qb, kr, att

    def back(rows, qb, kr, att):
        def v_of(h):
            return proj_ref[rows, 2 * RET_QK_DIM + RET_DV * h:2 * RET_QK_DIM + RET_DV * (h + 1)]

        o_st = [jnp.dot(qb[h], st_ref[h].astype(BF16), preferred_element_type=F32) for h in heads]
        for h in heads:
            o_att = jnp.dot((att[h] * dm_ref[h]).astype(BF16), v_of(h), preferred_element_type=F32)
            o = o_att + o_st[h] * jnp.concatenate([qd_ref[h]] * (RET_DV // q), axis=1)
            vs = slice(RET_DV * h, RET_DV * (h + 1))
            gate = proj_ref[rows, 2 * RET_QK_DIM + RET_V_DIM + RET_DV * h:
                            2 * RET_QK_DIM + RET_V_DIM + RET_DV * (h + 1)].astype(F32)
            o_ref[rows, vs] = (_silu(gate) * _rms_rows(o, nw_ref[:, vs])).astype(o_ref.dtype)
            kdec = (kr[h] * jnp.concatenate([kd_ref[h]] * (RET_DK // q), axis=1)).astype(BF16)
            st_ref[h] = st_ref[h] * math.exp(q * _RET_LOG_GAMMA[h]) + lax.dot_general(
                kdec, v_of(h), (((0,), (0,)), ((), ())), preferred_element_type=F32)

    chunk_rows = [slice(q * c, q * (c + 1)) for c in range(RET_CHUNKS_PER_STEP)]
    ready = front(chunk_rows[0])
    for c in range(RET_CHUNKS_PER_STEP):
        current = ready
        if c + 1 < RET_CHUNKS_PER_STEP:
            ready = front(chunk_rows[c + 1])
        back(chunk_rows[c], *current)


def _retention(proj, cos, sin, norm_w, batch, name):
    m = proj.shape[0]
    rows = CHUNK * RET_CHUNKS_PER_STEP
    nc = m // batch // rows
    row_blk = lambda b, c: (b * nc + c, 0)
    src = jnp.arange(RET_DK)[:, None]
    perm = (jnp.arange(RET_DK)[None, :] == (src % 2) * (RET_DK // 2) + src // 2).astype(BF16)
    return pl.pallas_call(
        _ret_kernel,
        grid=(batch, nc),
        in_specs=[pl.BlockSpec((rows, proj.shape[1]), row_blk),
                  pl.BlockSpec((rows, RET_DK // 2), lambda b, c: (c, 0)),
                  pl.BlockSpec((rows, RET_DK // 2), lambda b, c: (c, 0)),
                  pl.BlockSpec((RET_DK, RET_DK), lambda b, c: (0, 0)),
                  pl.BlockSpec((1, RET_V_DIM), lambda b, c: (0, 0))],
        out_specs=pl.BlockSpec((rows, RET_V_DIM), row_blk),
        out_shape=jax.ShapeDtypeStruct((m, RET_V_DIM), BF16),
        scratch_shapes=[pltpu.VMEM((RET_HEADS, RET_DK, RET_DV), F32),
                        pltpu.VMEM((RET_HEADS, CHUNK, CHUNK), F32),
                        pltpu.VMEM((RET_HEADS, CHUNK, CHUNK), F32),
                        pltpu.VMEM((RET_HEADS, CHUNK, CHUNK), F32)],
        compiler_params=_params("parallel", "arbitrary"),
        name=name,
    )(proj, cos, sin, perm, norm_w.reshape(1, -1))


def _ffn_hidden(x, nw, wg, wu, layer):
    return _ffn_in(x, nw, wg, wu, layer, 2048, 512, "ffn_in%d" % layer)


def kernel(x, norm_mix, ssm_w_in, ssm_conv_w, ssm_conv_b, ssm_dt_bias, ssm_a_log, ssm_d, ssm_norm, ssm_w_out, ret_w_in, ret_norm, ret_w_out, norm_ffn, ffn_w_gate, ffn_w_up, ffn_w_down, norm_final):
    batch, seq, d = x.shape
    m = batch * seq
    xf = x.reshape(m, d)
    wg, wu, wd = ffn_w_gate.astype(BF16), ffn_w_up.astype(BF16), ffn_w_down.astype(BF16)

    n_zx = SSM_D_INNER + SSM_CONV_DIM
    pad_h = (0, LANES - SSM_HEADS)
    w_dt = jnp.pad(ssm_w_in[0, :, n_zx:], ((0, 0), pad_h)).astype(BF16)
    zx, dt = _norm_matmul(xf, norm_mix[0], [ssm_w_in.astype(BF16)], 0, n_zx, BF16, 1024, 2048, "ssm_in",
                          w_side=w_dt)
    y = _ssd(zx, dt, ssm_conv_w[0], ssm_conv_b[0],
             jnp.pad(ssm_dt_bias[0].astype(F32), pad_h),
             jnp.pad(-jnp.exp(ssm_a_log[0].astype(F32)), pad_h),
             jnp.repeat(ssm_d[0].astype(F32), SSM_HEADDIM), ssm_norm[0], batch, "ssd")
    xf = _matmul_resid(y, ssm_w_out.astype(BF16), 0, xf, 1024, 1024, "ssm_out")
    hidden = _ffn_hidden(xf, norm_ffn[0], wg, wu, 0)
    xf = _matmul_resid(hidden, wd, 0, xf, 1024, 512, "ffn_out0")

    proj = _norm_matmul(xf, norm_mix[1], [ret_w_in.astype(BF16)], 0, ret_w_in.shape[2], BF16, 1024, 2048, "ret_in")
    freq = 1.0 / (ROPE_BASE ** jnp.linspace(0.0, 1.0, RET_DK // 2, dtype=F32))
    ang = jnp.arange(seq, dtype=F32)[:, None] * freq[None, :]
    o = _retention(proj, jnp.cos(ang), jnp.sin(ang), ret_norm[0], batch, "retention")
    xf = _matmul_resid(o, ret_w_out.astype(BF16), 0, xf, 1024, 1024, "ret_out")
    hidden = _ffn_hidden(xf, norm_ffn[1], wg, wu, 1)
    out = _matmul_resid_norm(hidden, wd, 1, xf, norm_final, 512, 1024, "ffn_out1_norm")
    return out.reshape(batch, seq, d)
```

```python
import functools
import math

import jax
import jax.numpy as jnp
from jax import lax
from jax.experimental import pallas as pl
from jax.experimental.pallas import tpu as pltpu

F32 = jnp.float32
BF16 = jnp.bfloat16

D_MODEL = 2048
NORM_EPS = 1e-6
CHUNK = 128

SSM_D_INNER = 4096
SSM_HEADDIM = 64
SSM_HEADS = 64
SSM_D_STATE = 128
SSM_GROUPS = 8
SSM_HPG = 8
SSM_CONV = 4
SSM_GN = SSM_GROUPS * SSM_D_STATE
SSM_CONV_DIM = SSM_D_INNER + 2 * SSM_GN
SSM_GROUP_W = SSM_HPG * SSM_HEADDIM
CONV_TAIL = 16
CONV_AHEAD = 2
LOG2E = 1.0 / math.log(2.0)

RET_HEADS = 8
RET_QK_DIM = 2048
RET_V_DIM = 4096
RET_DK = 256
RET_DV = 512
ROPE_BASE = 10000.0
RET_CHUNKS_PER_STEP = 2

LANES = 128
VMEM_LIMIT = 58 * 1024 * 1024


def _params(*sem):
    return pltpu.CompilerParams(dimension_semantics=sem, vmem_limit_bytes=VMEM_LIMIT)


def _silu(x):
    h = 0.5 * x
    return h + h * jnp.tanh(h)


def _rms_rows(x, w):
    ms = jnp.mean(x * x, axis=-1, keepdims=True)
    return x * lax.rsqrt(ms + NORM_EPS) * w


def _w_spec(k, tn, layer):
    return pl.BlockSpec((None, k, tn), lambda i, j: (layer, 0, j))


def _norm_matmul_kernel(x_ref, nw_ref, *refs, n_w, has_side, epilogue):
    w_refs = refs[:n_w]
    refs = refs[n_w:]
    if has_side:
        w2_ref, o_ref, o2_ref, h_ref, inv_ref = refs
    else:
        o_ref, h_ref, inv_ref = refs
    j = pl.program_id(1)

    def emit(h, inv):
        o_ref[...] = epilogue(*[jnp.dot(h, w[...], preferred_element_type=F32) * inv
                                for w in w_refs]).astype(o_ref.dtype)

    @pl.when(j == 0)
    def _():
        x = x_ref[...]
        inv = lax.rsqrt(jnp.mean(x * x, axis=-1, keepdims=True) + NORM_EPS)
        inv_ref[...] = jnp.broadcast_to(inv, inv_ref.shape)
        h = (x * nw_ref[...]).astype(BF16)
        h_ref[...] = h
        emit(h, inv)
        if has_side:
            o2_ref[...] = jnp.dot(h, w2_ref[...], preferred_element_type=F32) * inv

    @pl.when(j > 0)
    def _():
        emit(h_ref[...], inv_ref[:, 0:1])


def _norm_matmul(x, nw, ws, layer, n, out_dtype, tm, tn, name, epilogue=lambda y: y, w_side=None):
    m, k = x.shape
    in_specs = [pl.BlockSpec((tm, k), lambda i, j: (i, 0)),
                pl.BlockSpec((1, k), lambda i, j: (0, 0))] + [_w_spec(k, tn, layer) for _ in ws]
    out_specs = pl.BlockSpec((tm, tn), lambda i, j: (i, j))
    out_shape = jax.ShapeDtypeStruct((m, n), out_dtype)
    operands = [x, nw.reshape(1, k), *ws]
    if w_side is not None:
        n2 = w_side.shape[1]
        in_specs.append(pl.BlockSpec((k, n2), lambda i, j: (0, 0)))
        out_specs = [out_specs, pl.BlockSpec((tm, n2), lambda i, j: (i, 0))]
        out_shape = [out_shape, jax.ShapeDtypeStruct((m, n2), F32)]
        operands.append(w_side)
    return pl.pallas_call(
        functools.partial(_norm_matmul_kernel, n_w=len(ws), has_side=w_side is not None, epilogue=epilogue),
        grid=(m // tm, n // tn),
        in_specs=in_specs,
        out_specs=out_specs,
        out_shape=out_shape,
        scratch_shapes=[pltpu.VMEM((tm, k), BF16), pltpu.VMEM((tm, LANES), F32)],
        compiler_params=_params("parallel", "arbitrary"),
        name=name,
    )(*operands)


def _ffn_in_kernel(x_hbm, nw_ref, wg_ref, wu_ref, o_ref, x_buf, h_ref, inv_ref, x_sem):
    i = pl.program_id(0)
    j = pl.program_id(1)
    tm = x_buf.shape[0]

    def x_copy(tile):
        return pltpu.make_async_copy(x_hbm.at[pl.ds(tile * tm, tm), :], x_buf, x_sem)

    def emit(h, inv):
        g = jnp.dot(h, wg_ref[...], preferred_element_type=F32) * inv
        u = jnp.dot(h, wu_ref[...], preferred_element_type=F32) * inv
        o_ref[...] = (_silu(g) * u).astype(o_ref.dtype)

    @pl.when(jnp.logical_and(i == 0, j == 0))
    def _():
        x_copy(0).start()

    @pl.when(j == 0)
    def _():
        x_copy(i).wait()
        x = x_buf[...]
        inv = lax.rsqrt(jnp.mean(x * x, axis=-1, keepdims=True) + NORM_EPS)
        inv_ref[...] = jnp.broadcast_to(inv, inv_ref.shape)
        h = (x * nw_ref[...]).astype(BF16)
        h_ref[...] = h
        emit(h, inv)

    @pl.when(jnp.logical_and(j == 1, i + 1 < pl.num_programs(0)))
    def _():
        x_copy(i + 1).start()

    @pl.when(j > 0)
    def _():
        emit(h_ref[...], inv_ref[:, 0:1])


def _ffn_in(x, nw, wg, wu, layer, tm, tn, name):
    m, k = x.shape
    n = wg.shape[2]
    assert n // tn >= 2
    return pl.pallas_call(
        _ffn_in_kernel,
        grid=(m // tm, n // tn),
        in_specs=[pl.BlockSpec(memory_space=pl.ANY),
                  pl.BlockSpec((1, k), lambda i, j: (0, 0)),
                  _w_spec(k, tn, layer),
                  _w_spec(k, tn, layer)],
        out_specs=pl.BlockSpec((tm, tn), lambda i, j: (i, j)),
        out_shape=jax.ShapeDtypeStruct((m, n), BF16),
        scratch_shapes=[pltpu.VMEM((tm, k), F32), pltpu.VMEM((tm, k), BF16), pltpu.VMEM((tm, LANES), F32),
                        pltpu.SemaphoreType.DMA(())],
        compiler_params=_params("arbitrary", "arbitrary"),
        name=name,
    )(x, nw.reshape(1, k), wg, wu)


def _matmul_resid_kernel(a_ref, w_ref, r_ref, *refs, has_norm):
    o_ref = refs[-1]
    val = r_ref[...] + jnp.dot(a_ref[...], w_ref[...], preferred_element_type=F32)
    o_ref[...] = _rms_rows(val, refs[0][...]) if has_norm else val


def _matmul_resid(a, w, layer, resid, tm, name, norm_w=None):
    m, k = a.shape
    n = w.shape[2]
    in_specs = [pl.BlockSpec((tm, k), lambda i: (i, 0)),
                pl.BlockSpec((None, k, n), lambda i: (layer, 0, 0), pipeline_mode=pl.Buffered(1)),
                pl.BlockSpec((tm, n), lambda i: (i, 0))]
    operands = [a, w, resid]
    if norm_w is not None:
        in_specs.append(pl.BlockSpec((1, n), lambda i: (0, 0)))
        operands.append(norm_w.reshape(1, n))
    return pl.pallas_call(
        functools.partial(_matmul_resid_kernel, has_norm=norm_w is not None),
        grid=(m // tm,),
        in_specs=in_specs,
        out_specs=pl.BlockSpec((tm, n), lambda i: (i, 0)),
        out_shape=jax.ShapeDtypeStruct((m, n), F32),
        compiler_params=_params("parallel"),
        name=name,
    )(*operands)


def _ssd_kernel(zx_ref, dt_ref, shift_ref, cw_ref, cb_ref, dtb_ref, aneg_ref, dskip_ref, nw_ref,
                y_ref, xpad_ref, st_ref):
    q = CHUNK

    @pl.when(pl.program_id(1) == 0)
    def _():
        xpad_ref[0:CONV_TAIL, :] = jnp.zeros((CONV_TAIL, SSM_CONV_DIM), BF16)
        st_ref[...] = jnp.zeros_like(st_ref)

    xpad_ref[CONV_TAIL:CONV_TAIL + q, :] = zx_ref[:, SSM_D_INNER:]

    row = lax.broadcasted_iota(jnp.int32, (q, q), 0)
    col = lax.broadcasted_iota(jnp.int32, (q, q), 1)
    causal = row >= col
    neg_mask = jnp.where(causal, 0.0, -jnp.inf)
    lo = col < SSM_HEADDIM

    dtr = dt_ref[...] + dtb_ref[...]
    dt = jnp.maximum(dtr, 0.0) + jnp.log1p(jnp.exp(-jnp.abs(dtr)))
    a = dt * aneg_ref[...]
    cum = jnp.dot(causal.astype(F32), a, precision=lax.Precision.HIGHEST,
                  preferred_element_type=F32) * LOG2E
    cum_t = cum.T
    dt_t = dt.T
    to_end_t = jnp.exp2(cum_t[:, q - 1:q] - cum_t) * dt_t

    def conv_offsets(g):
        return ((SSM_GROUP_W * g, SSM_GROUP_W),
                (SSM_D_INNER + SSM_D_STATE * g, SSM_D_STATE),
                (SSM_D_INNER + SSM_GN + SSM_D_STATE * g, SSM_D_STATE))

    def conv_shift(g):
        return [jnp.dot(shift_ref[...], xpad_ref[:, off:off + width], preferred_element_type=F32)
                for off, width in conv_offsets(g)]

    def conv_finish(g, shifted):
        outs = []
        for (off, width), sh in zip(conv_offsets(g), shifted):
            cur = zx_ref[:, SSM_D_INNER + off:SSM_D_INNER + off + width].astype(F32)
            acc = cb_ref[:, off:off + width] + cw_ref[SSM_CONV - 1:SSM_CONV, off:off + width] * cur
            for k in range(SSM_CONV - 1):
                acc = acc + cw_ref[k:k + 1, off:off + width] * sh[k * q:(k + 1) * q]
            outs.append(_silu(acc))
        return outs

    def prepare(g, shifted):
        xg, bg, cg = conv_finish(g, shifted)
        cb = cg.astype(BF16)
        cbm = lax.dot_general(cb, bg.astype(BF16), (((1,), (1,)), ((), ())),
                              preferred_element_type=F32)
        st = st_ref[g]
        y_inter = jnp.dot(cb, st.astype(BF16), preferred_element_type=F32)
        return xg, bg.T, cbm, st, y_inter

    def scan(g, xg, bg_t, cbm, st, y_inter):
        gx = SSM_GROUP_W * g
        y_pairs = []
        for p in range(SSM_HPG // 2):
            cols = slice(LANES * p, LANES * (p + 1))
            xp = xg[:, cols]
            x2 = jnp.concatenate([jnp.where(lo, xp, 0.0), jnp.where(lo, 0.0, xp)], axis=0).astype(BF16)
            wgts, bts, cis, decs = [], [], [], []
            for hh in (SSM_HPG * g + 2 * p, SSM_HPG * g + 2 * p + 1):
                ci = jnp.broadcast_to(cum[:, hh:hh + 1], (q, q))
                cj = cum_t[hh:hh + 1, :]
                decay = jnp.exp2(ci - cj + neg_mask)
                wgts.append((decay * cbm * dt_t[hh:hh + 1, :]).astype(BF16))
                bts.append((bg_t * to_end_t[hh:hh + 1, :]).astype(BF16))
                cis.append(ci)
                decs.append(jnp.exp2(cum_t[hh:hh + 1, q - 1:q]))
            y_intra = jnp.dot(jnp.concatenate(wgts, axis=1), x2, preferred_element_type=F32)
            y_pairs.append(y_intra + y_inter[:, cols] * jnp.exp2(jnp.where(lo, cis[0], cis[1])))
            d_st = jnp.dot(jnp.concatenate(bts, axis=1), x2, preferred_element_type=F32)
            dec = jnp.where(lo[0:1, :], decs[0], decs[1])
            st_ref[g, :, cols] = st[:, cols] * dec + d_st

        yg = jnp.concatenate(y_pairs, axis=1)
        yg = yg + dskip_ref[:, gx:gx + SSM_GROUP_W] * xg
        yg = yg * _silu(zx_ref[:, gx:gx + SSM_GROUP_W].astype(F32))
        y_ref[:, gx:gx + SSM_GROUP_W] = _rms_rows(yg, nw_ref[:, gx:gx + SSM_GROUP_W]).astype(y_ref.dtype)

    shifted = {g: conv_shift(g) for g in range(CONV_AHEAD)}
    ready = prepare(0, shifted.pop(0))
    for g in range(SSM_GROUPS):
        current = ready
        if g + CONV_AHEAD < SSM_GROUPS:
            shifted[g + CONV_AHEAD] = conv_shift(g + CONV_AHEAD)
        if g + 1 < SSM_GROUPS:
            ready = prepare(g + 1, shifted.pop(g + 1))
        scan(g, *current)

    xpad_ref[0:CONV_TAIL, :] = xpad_ref[q:q + CONV_TAIL, :]


def _ssd(zx, dt, conv_w, conv_b, dt_bias, a_neg, d_skip, norm_w, batch, name):
    m = zx.shape[0]
    nc = m // batch // CHUNK
    row_blk = lambda b, c: (b * nc + c, 0)
    const = lambda b, c: (0, 0)
    out_row = jnp.arange((SSM_CONV - 1) * CHUNK)[:, None]
    src_row = CONV_TAIL + out_row % CHUNK - (SSM_CONV - 1) + out_row // CHUNK
    shift = (jnp.arange(CONV_TAIL + CHUNK)[None, :] == src_row).astype(BF16)
    return pl.pallas_call(
        _ssd_kernel,
        grid=(batch, nc),
        in_specs=[pl.BlockSpec((CHUNK, zx.shape[1]), row_blk),
                  pl.BlockSpec((CHUNK, LANES), row_blk),
                  pl.BlockSpec(((SSM_CONV - 1) * CHUNK, CONV_TAIL + CHUNK), const),
                  pl.BlockSpec((SSM_CONV, SSM_CONV_DIM), const),
                  pl.BlockSpec((1, SSM_CONV_DIM), const),
                  pl.BlockSpec((1, LANES), const),
                  pl.BlockSpec((1, LANES), const),
                  pl.BlockSpec((1, SSM_D_INNER), const),
                  pl.BlockSpec((1, SSM_D_INNER), const)],
        out_specs=pl.BlockSpec((CHUNK, SSM_D_INNER), row_blk),
        out_shape=jax.ShapeDtypeStruct((m, SSM_D_INNER), BF16),
        scratch_shapes=[pltpu.VMEM((CONV_TAIL + CHUNK, SSM_CONV_DIM), BF16),
                        pltpu.VMEM((SSM_GROUPS, SSM_D_STATE, SSM_GROUP_W), F32)],
        compiler_params=_params("parallel", "arbitrary"),
        name=name,
    )(zx, dt, shift, conv_w, conv_b.reshape(1, -1), dt_bias.reshape(1, -1), a_neg.reshape(1, -1),
      d_skip.reshape(1, -1), norm_w.reshape(1, -1))


_RET_LOG_GAMMA = [math.log(1.0 - 2.0 ** (-5.0 - h)) for h in range(RET_HEADS)]


def _ret_kernel(proj_ref, cos_ref, sin_ref, perm_ref, nw_ref, o_ref, st_ref, dm_ref, qd_ref, kd_ref):
    q = CHUNK
    k_scale = RET_DK ** -0.5

    @pl.when(pl.program_id(1) == 0)
    def _():
        st_ref[...] = jnp.zeros_like(st_ref)
        row = lax.broadcasted_iota(jnp.int32, (q, q), 0)
        col = lax.broadcasted_iota(jnp.int32, (q, q), 1)
        diff = (row - col).astype(F32)
        rowf = row.astype(F32)
        for h in range(RET_HEADS):
            lg = _RET_LOG_GAMMA[h]
            dm_ref[h] = jnp.exp(jnp.where(diff >= 0, diff * lg, -jnp.inf)) * k_scale
            qd_ref[h] = jnp.exp((rowf + 1.0) * lg)
            kd_ref[h] = jnp.exp((q - 1.0 - rowf) * lg) * k_scale

    half = RET_DK // 2
    heads = range(RET_HEADS)

    def front(rows):
        cos = cos_ref[rows, :]
        sin = sin_ref[rows, :]

        def deinterleave(off):
            return jnp.dot(proj_ref[rows, off:off + RET_DK], perm_ref[...], preferred_element_type=F32)

        def rotary(t):
            t1, t2 = t[:, :half], t[:, half:]
            return jnp.concatenate([t1 * cos - t2 * sin, t1 * sin + t2 * cos], axis=1)

        tq = [deinterleave(RET_DK * h) for h in heads]
        tk = [deinterleave(RET_QK_DIM + RET_DK * h) for h in heads]
        qb = [rotary(t).astype(BF16) for t in tq]
        kr = [rotary(t) for t in tk]
        att = [lax.dot_general(qb[h], kr[h].astype(BF16), (((1,), (1,)), ((), ())),
                               preferred_element_type=F32) for h in heads]
        return qb, kr, att

    def back(rows, qb, kr, att):
        def v_of(h):
            return proj_ref[rows, 2 * RET_QK_DIM + RET_DV * h:2 * RET_QK_DIM + RET_DV * (h + 1)]

        o_st = [jnp.dot(qb[h], st_ref[h].astype(BF16), preferred_element_type=F32) for h in heads]
        for h in heads:
            o_att = jnp.dot((att[h] * dm_ref[h]).astype(BF16), v_of(h), preferred_element_type=F32)
            o = o_att + o_st[h] * jnp.concatenate([qd_ref[h]] * (RET_DV // q), axis=1)
            vs = slice(RET_DV * h, RET_DV * (h + 1))
            gate = proj_ref[rows, 2 * RET_QK_DIM + RET_V_DIM + RET_DV * h:
                            2 * RET_QK_DIM + RET_V_DIM + RET_DV * (h + 1)].astype(F32)
            o_ref[rows, vs] = (_silu(gate) * _rms_rows(o, nw_ref[:, vs])).astype(o_ref.dtype)
            kdec = (kr[h] * jnp.concatenate([kd_ref[h]] * (RET_DK // q), axis=1)).astype(BF16)
            st_ref[h] = st_ref[h] * math.exp(q * _RET_LOG_GAMMA[h]) + lax.dot_general(
                kdec, v_of(h), (((0,), (0,)), ((), ())), preferred_element_type=F32)

    chunk_rows = [slice(q * c, q * (c + 1)) for c in range(RET_CHUNKS_PER_STEP)]
    ready = front(chunk_rows[0])
    for c in range(RET_CHUNKS_PER_STEP):
        current = ready
        if c + 1 < RET_CHUNKS_PER_STEP:
            ready = front(chunk_rows[c + 1])
        back(chunk_rows[c], *current)


def _retention(proj, cos, sin, norm_w, batch, name):
    m = proj.shape[0]
    rows = CHUNK * RET_CHUNKS_PER_STEP
    nc = m // batch // rows
    row_blk = lambda b, c: (b * nc + c, 0)
    src = jnp.arange(RET_DK)[:, None]
    perm = (jnp.arange(RET_DK)[None, :] == (src % 2) * (RET_DK // 2) + src // 2).astype(BF16)
    return pl.pallas_call(
        _ret_kernel,
        grid=(batch, nc),
        in_specs=[pl.BlockSpec((rows, proj.shape[1]), row_blk),
                  pl.BlockSpec((rows, RET_DK // 2), lambda b, c: (c, 0)),
                  pl.BlockSpec((rows, RET_DK // 2), lambda b, c: (c, 0)),
                  pl.BlockSpec((RET_DK, RET_DK), lambda b, c: (0, 0)),
                  pl.BlockSpec((1, RET_V_DIM), lambda b, c: (0, 0))],
        out_specs=pl.BlockSpec((rows, RET_V_DIM), row_blk),
        out_shape=jax.ShapeDtypeStruct((m, RET_V_DIM), BF16),
        scratch_shapes=[pltpu.VMEM((RET_HEADS, RET_DK, RET_DV), F32),
                        pltpu.VMEM((RET_HEADS, CHUNK, CHUNK), F32),
                        pltpu.VMEM((RET_HEADS, CHUNK, CHUNK), F32),
                        pltpu.VMEM((RET_HEADS, CHUNK, CHUNK), F32)],
        compiler_params=_params("parallel", "arbitrary"),
        name=name,
    )(proj, cos, sin, perm, norm_w.reshape(1, -1))


def _ffn_hidden(x, nw, wg, wu, layer):
    return _ffn_in(x, nw, wg, wu, layer, 2048, 512, "ffn_in%d" % layer)


def kernel(x, norm_mix, ssm_w_in, ssm_conv_w, ssm_conv_b, ssm_dt_bias, ssm_a_log, ssm_d, ssm_norm, ssm_w_out, ret_w_in, ret_norm, ret_w_out, norm_ffn, ffn_w_gate, ffn_w_up, ffn_w_down, norm_final):
    batch, seq, d = x.shape
    m = batch * seq
    xf = x.reshape(m, d)
    wg, wu, wd = ffn_w_gate.astype(BF16), ffn_w_up.astype(BF16), ffn_w_down.astype(BF16)

    n_zx = SSM_D_INNER + SSM_CONV_DIM
    pad_h = (0, LANES - SSM_HEADS)
    w_dt = jnp.pad(ssm_w_in[0, :, n_zx:], ((0, 0), pad_h)).astype(BF16)
    zx, dt = _norm_matmul(xf, norm_mix[0], [ssm_w_in.astype(BF16)], 0, n_zx, BF16, 1024, 2048, "ssm_in",
                          w_side=w_dt)
    y = _ssd(zx, dt, ssm_conv_w[0], ssm_conv_b[0],
             jnp.pad(ssm_dt_bias[0].astype(F32), pad_h),
             jnp.pad(-jnp.exp(ssm_a_log[0].astype(F32)), pad_h),
             jnp.repeat(ssm_d[0].astype(F32), SSM_HEADDIM), ssm_norm[0], batch, "ssd")
    xf = _matmul_resid(y, ssm_w_out.astype(BF16), 0, xf, 512, "ssm_out")
    hidden = _ffn_hidden(xf, norm_ffn[0], wg, wu, 0)
    xf = _matmul_resid(hidden, wd, 0, xf, 512, "ffn_out0")

    proj = _norm_matmul(xf, norm_mix[1], [ret_w_in.astype(BF16)], 0, ret_w_in.shape[2], BF16, 1024, 2048, "ret_in")
    freq = 1.0 / (ROPE_BASE ** jnp.linspace(0.0, 1.0, RET_DK // 2, dtype=F32))
    ang = jnp.arange(seq, dtype=F32)[:, None] * freq[None, :]
    o = _retention(proj, jnp.cos(ang), jnp.sin(ang), ret_norm[0], batch, "retention")
    xf = _matmul_resid(o, ret_w_out.astype(BF16), 0, xf, 512, "ret_out")
    hidden = _ffn_hidden(xf, norm_ffn[1], wg, wu, 1)
    out = _matmul_resid(hidden, wd, 1, xf, 512, "ffn_out1_norm", norm_w=norm_final)
    return out.reshape(batch, seq, d)
```

```python
import functools
import math

import jax
import jax.numpy as jnp
from jax import lax
from jax.experimental import pallas as pl
from jax.experimental.pallas import tpu as pltpu

F32 = jnp.float32
BF16 = jnp.bfloat16

D_MODEL = 2048
NORM_EPS = 1e-6
CHUNK = 128

SSM_D_INNER = 4096
SSM_HEADDIM = 64
SSM_HEADS = 64
SSM_D_STATE = 128
SSM_GROUPS = 8
SSM_HPG = 8
SSM_CONV = 4
SSM_GN = SSM_GROUPS * SSM_D_STATE
SSM_CONV_DIM = SSM_D_INNER + 2 * SSM_GN
SSM_GROUP_W = SSM_HPG * SSM_HEADDIM
CONV_TAIL = 16
CONV_AHEAD = 2
LOG2E = 1.0 / math.log(2.0)

RET_HEADS = 8
RET_QK_DIM = 2048
RET_V_DIM = 4096
RET_DK = 256
RET_DV = 512
ROPE_BASE = 10000.0
RET_CHUNKS_PER_STEP = 2

LANES = 128
VMEM_LIMIT = 58 * 1024 * 1024


def _params(*sem):
    return pltpu.CompilerParams(dimension_semantics=sem, vmem_limit_bytes=VMEM_LIMIT)


def _silu(x):
    h = 0.5 * x
    return h + h * jnp.tanh(h)


def _rms_rows(x, w):
    ms = jnp.mean(x * x, axis=-1, keepdims=True)
    return x * lax.rsqrt(ms + NORM_EPS) * w


def _w_spec(k, tn, layer):
    return pl.BlockSpec((None, k, tn), lambda i, j: (layer, 0, j))


def _norm_matmul_kernel(x_ref, nw_ref, *refs, n_w, has_side, epilogue):
    w_refs = refs[:n_w]
    refs = refs[n_w:]
    if has_side:
        w2_ref, o_ref, o2_ref, h_ref, inv_ref = refs
    else:
        o_ref, h_ref, inv_ref = refs
    j = pl.program_id(1)

    def emit(h, inv):
        o_ref[...] = epilogue(*[jnp.dot(h, w[...], preferred_element_type=F32) * inv
                                for w in w_refs]).astype(o_ref.dtype)

    @pl.when(j == 0)
    def _():
        x = x_ref[...]
        inv = lax.rsqrt(jnp.mean(x * x, axis=-1, keepdims=True) + NORM_EPS)
        inv_ref[...] = jnp.broadcast_to(inv, inv_ref.shape)
        h = (x * nw_ref[...]).astype(BF16)
        h_ref[...] = h
        emit(h, inv)
        if has_side:
            o2_ref[...] = jnp.dot(h, w2_ref[...], preferred_element_type=F32) * inv

    @pl.when(j > 0)
    def _():
        emit(h_ref[...], inv_ref[:, 0:1])


def _norm_matmul(x, nw, ws, layer, n, out_dtype, tm, tn, name, epilogue=lambda y: y, w_side=None):
    m, k = x.shape
    in_specs = [pl.BlockSpec((tm, k), lambda i, j: (i, 0)),
                pl.BlockSpec((1, k), lambda i, j: (0, 0))] + [_w_spec(k, tn, layer) for _ in ws]
    out_specs = pl.BlockSpec((tm, tn), lambda i, j: (i, j))
    out_shape = jax.ShapeDtypeStruct((m, n), out_dtype)
    operands = [x, nw.reshape(1, k), *ws]
    if w_side is not None:
        n2 = w_side.shape[1]
        in_specs.append(pl.BlockSpec((k, n2), lambda i, j: (0, 0)))
        out_specs = [out_specs, pl.BlockSpec((tm, n2), lambda i, j: (i, 0))]
        out_shape = [out_shape, jax.ShapeDtypeStruct((m, n2), F32)]
        operands.append(w_side)
    return pl.pallas_call(
        functools.partial(_norm_matmul_kernel, n_w=len(ws), has_side=w_side is not None, epilogue=epilogue),
        grid=(m // tm, n // tn),
        in_specs=in_specs,
        out_specs=out_specs,
        out_shape=out_shape,
        scratch_shapes=[pltpu.VMEM((tm, k), BF16), pltpu.VMEM((tm, LANES), F32)],
        compiler_params=_params("parallel", "arbitrary"),
        name=name,
    )(*operands)


def _ffn_in_kernel(x_hbm, nw_ref, wg_ref, wu_ref, o_ref, x_buf, h_ref, inv_ref, x_sem):
    i = pl.program_id(0)
    j = pl.program_id(1)
    tm = x_buf.shape[0]

    def x_copy(tile):
        return pltpu.make_async_copy(x_hbm.at[pl.ds(tile * tm, tm), :], x_buf, x_sem)

    def emit(h, inv):
        g = jnp.dot(h, wg_ref[...], preferred_element_type=F32) * inv
        u = jnp.dot(h, wu_ref[...], preferred_element_type=F32) * inv
        o_ref[...] = (_silu(g) * u).astype(o_ref.dtype)

    @pl.when(jnp.logical_and(i == 0, j == 0))
    def _():
        x_copy(0).start()

    @pl.when(j == 0)
    def _():
        x_copy(i).wait()
        x = x_buf[...]
        inv = lax.rsqrt(jnp.mean(x * x, axis=-1, keepdims=True) + NORM_EPS)
        inv_ref[...] = jnp.broadcast_to(inv, inv_ref.shape)
        h = (x * nw_ref[...]).astype(BF16)
        h_ref[...] = h
        emit(h, inv)

    @pl.when(jnp.logical_and(j == 1, i + 1 < pl.num_programs(0)))
    def _():
        x_copy(i + 1).start()

    @pl.when(j > 0)
    def _():
        emit(h_ref[...], inv_ref[:, 0:1])


def _ffn_in(x, nw, wg, wu, layer, tm, tn, name):
    m, k = x.shape
    n = wg.shape[2]
    assert n // tn >= 2
    return pl.pallas_call(
        _ffn_in_kernel,
        grid=(m // tm, n // tn),
        in_specs=[pl.BlockSpec(memory_space=pl.ANY),
                  pl.BlockSpec((1, k), lambda i, j: (0, 0)),
                  _w_spec(k, tn, layer),
                  _w_spec(k, tn, layer)],
        out_specs=pl.BlockSpec((tm, tn), lambda i, j: (i, j)),
        out_shape=jax.ShapeDtypeStruct((m, n), BF16),
        scratch_shapes=[pltpu.VMEM((tm, k), F32), pltpu.VMEM((tm, k), BF16), pltpu.VMEM((tm, LANES), F32),
                        pltpu.SemaphoreType.DMA(())],
        compiler_params=_params("arbitrary", "arbitrary"),
        name=name,
    )(x, nw.reshape(1, k), wg, wu)


def _matmul_resid_kernel(a_ref, w_ref, r_ref, *refs, has_norm):
    o_ref = refs[-1]
    val = r_ref[...] + jnp.dot(a_ref[...], w_ref[...], preferred_element_type=F32)
    o_ref[...] = _rms_rows(val, refs[0][...]) if has_norm else val


def _matmul_resid(a, w, layer, resid, tm, name, norm_w=None):
    m, k = a.shape
    n = w.shape[2]
    in_specs = [pl.BlockSpec((tm, k), lambda i: (i, 0)),
                pl.BlockSpec((None, k, n), lambda i: (layer, 0, 0), pipeline_mode=pl.Buffered(1)),
                pl.BlockSpec((tm, n), lambda i: (i, 0))]
    operands = [a, w, resid]
    if norm_w is not None:
        in_specs.append(pl.BlockSpec((1, n), lambda i: (0, 0)))
        operands.append(norm_w.reshape(1, n))
    return pl.pallas_call(
        functools.partial(_matmul_resid_kernel, has_norm=norm_w is not None),
        grid=(m // tm,),
        in_specs=in_specs,
        out_specs=pl.BlockSpec((tm, n), lambda i: (i, 0)),
        out_shape=jax.ShapeDtypeStruct((m, n), F32),
        compiler_params=_params("parallel"),
        name=name,
    )(*operands)


def _ssd_kernel(zx_ref, dt_ref, shift_ref, cw_ref, cb_ref, dtb_ref, aneg_ref, dskip_ref, nw_ref,
                y_ref, xpad_ref, st_ref):
    q = CHUNK

    @pl.when(pl.program_id(1) == 0)
    def _():
        xpad_ref[0:CONV_TAIL, :] = jnp.zeros((CONV_TAIL, SSM_CONV_DIM), BF16)
        st_ref[...] = jnp.zeros_like(st_ref)

    xpad_ref[CONV_TAIL:CONV_TAIL + q, :] = zx_ref[:, SSM_D_INNER:]

    row = lax.broadcasted_iota(jnp.int32, (q, q), 0)
    col = lax.broadcasted_iota(jnp.int32, (q, q), 1)
    causal = row >= col
    neg_mask = jnp.where(causal, 0.0, -jnp.inf)
    lo = col < SSM_HEADDIM

    dtr = dt_ref[...] + dtb_ref[...]
    dt = jnp.maximum(dtr, 0.0) + jnp.log1p(jnp.exp(-jnp.abs(dtr)))
    a = dt * aneg_ref[...]
    cum = jnp.dot(causal.astype(F32), a, precision=lax.Precision.HIGHEST,
                  preferred_element_type=F32) * LOG2E
    cum_t = cum.T
    dt_t = dt.T
    to_end_t = jnp.exp2(cum_t[:, q - 1:q] - cum_t) * dt_t

    def conv_offsets(g):
        return ((SSM_GROUP_W * g, SSM_GROUP_W),
                (SSM_D_INNER + SSM_D_STATE * g, SSM_D_STATE),
                (SSM_D_INNER + SSM_GN + SSM_D_STATE * g, SSM_D_STATE))

    def conv_shift(g):
        return [jnp.dot(shift_ref[...], xpad_ref[:, off:off + width], preferred_element_type=F32)
                for off, width in conv_offsets(g)]

    def conv_finish(g, shifted):
        outs = []
        for (off, width), sh in zip(conv_offsets(g), shifted):
            cur = zx_ref[:, SSM_D_INNER + off:SSM_D_INNER + off + width].astype(F32)
            acc = cb_ref[:, off:off + width] + cw_ref[SSM_CONV - 1:SSM_CONV, off:off + width] * cur
            for k in range(SSM_CONV - 1):
                acc = acc + cw_ref[k:k + 1, off:off + width] * sh[k * q:(k + 1) * q]
            outs.append(_silu(acc))
        return outs

    def prepare(g, shifted):
        xg, bg, cg = conv_finish(g, shifted)
        cb = cg.astype(BF16)
        cbm = lax.dot_general(cb, bg.astype(BF16), (((1,), (1,)), ((), ())),
                              preferred_element_type=F32)
        st = st_ref[g]
        y_inter = jnp.dot(cb, st.astype(BF16), preferred_element_type=F32)
        return xg, bg.T, cbm, st, y_inter

    def scan(g, xg, bg_t, cbm, st, y_inter):
        gx = SSM_GROUP_W * g
        y_pairs = []
        for p in range(SSM_HPG // 2):
            cols = slice(LANES * p, LANES * (p + 1))
            xp = xg[:, cols]
            x2 = jnp.concatenate([jnp.where(lo, xp, 0.0), jnp.where(lo, 0.0, xp)], axis=0).astype(BF16)
            wgts, bts, cis, decs = [], [], [], []
            for hh in (SSM_HPG * g + 2 * p, SSM_HPG * g + 2 * p + 1):
                ci = jnp.broadcast_to(cum[:, hh:hh + 1], (q, q))
                cj = cum_t[hh:hh + 1, :]
                decay = jnp.exp2(ci - cj + neg_mask)
                wgts.append((decay * cbm * dt_t[hh:hh + 1, :]).astype(BF16))
                bts.append((bg_t * to_end_t[hh:hh + 1, :]).astype(BF16))
                cis.append(ci)
                decs.append(jnp.exp2(cum_t[hh:hh + 1, q - 1:q]))
            y_intra = jnp.dot(jnp.concatenate(wgts, axis=1), x2, preferred_element_type=F32)
            y_pairs.append(y_intra + y_inter[:, cols] * jnp.exp2(jnp.where(lo, cis[0], cis[1])))
            d_st = jnp.dot(jnp.concatenate(bts, axis=1), x2, preferred_element_type=F32)
            dec = jnp.where(lo[0:1, :], decs[0], decs[1])
            st_ref[g, :, cols] = st[:, cols] * dec + d_st

        yg = jnp.concatenate(y_pairs, axis=1)
        yg = yg + dskip_ref[:, gx:gx + SSM_GROUP_W] * xg
        yg = yg * _silu(zx_ref[:, gx:gx + SSM_GROUP_W].astype(F32))
        y_ref[:, gx:gx + SSM_GROUP_W] = _rms_rows(yg, nw_ref[:, gx:gx + SSM_GROUP_W]).astype(y_ref.dtype)

    shifted = {g: conv_shift(g) for g in range(CONV_AHEAD)}
    ready = prepare(0, shifted.pop(0))
    for g in range(SSM_GROUPS):
        current = ready
        if g + CONV_AHEAD < SSM_GROUPS:
            shifted[g + CONV_AHEAD] = conv_shift(g + CONV_AHEAD)
        if g + 1 < SSM_GROUPS:
            ready = prepare(g + 1, shifted.pop(g + 1))
        scan(g, *current)

    xpad_ref[0:CONV_TAIL, :] = xpad_ref[q:q + CONV_TAIL, :]


def _ssd(zx, dt, conv_w, conv_b, dt_bias, a_neg, d_skip, norm_w, batch, name):
    m = zx.shape[0]
    nc = m // batch // CHUNK
    row_blk = lambda b, c: (b * nc + c, 0)
    const = lambda b, c: (0, 0)
    out_row = jnp.arange((SSM_CONV - 1) * CHUNK)[:, None]
    src_row = CONV_TAIL + out_row % CHUNK - (SSM_CONV - 1) + out_row // CHUNK
    shift = (jnp.arange(CONV_TAIL + CHUNK)[None, :] == src_row).astype(BF16)
    return pl.pallas_call(
        _ssd_kernel,
        grid=(batch, nc),
        in_specs=[pl.BlockSpec((CHUNK, zx.shape[1]), row_blk),
                  pl.BlockSpec((CHUNK, LANES), row_blk),
                  pl.BlockSpec(((SSM_CONV - 1) * CHUNK, CONV_TAIL + CHUNK), const),
                  pl.BlockSpec((SSM_CONV, SSM_CONV_DIM), const),
                  pl.BlockSpec((1, SSM_CONV_DIM), const),
                  pl.BlockSpec((1, LANES), const),
                  pl.BlockSpec((1, LANES), const),
                  pl.BlockSpec((1, SSM_D_INNER), const),
                  pl.BlockSpec((1, SSM_D_INNER), const)],
        out_specs=pl.BlockSpec((CHUNK, SSM_D_INNER), row_blk),
        out_shape=jax.ShapeDtypeStruct((m, SSM_D_INNER), BF16),
        scratch_shapes=[pltpu.VMEM((CONV_TAIL + CHUNK, SSM_CONV_DIM), BF16),
                        pltpu.VMEM((SSM_GROUPS, SSM_D_STATE, SSM_GROUP_W), F32)],
        compiler_params=_params("parallel", "arbitrary"),
        name=name,
    )(zx, dt, shift, conv_w, conv_b.reshape(1, -1), dt_bias.reshape(1, -1), a_neg.reshape(1, -1),
      d_skip.reshape(1, -1), norm_w.reshape(1, -1))


_RET_LOG_GAMMA = [math.log(1.0 - 2.0 ** (-5.0 - h)) for h in range(RET_HEADS)]


def _ret_kernel(proj_ref, cos_ref, sin_ref, cos_t_ref, sin_t_ref, perm_ref, perm_t_ref, nw_ref, o_ref,
                st_ref, dm_ref, qd_ref, kd_ref):
    q = CHUNK
    k_scale = RET_DK ** -0.5

    @pl.when(pl.program_id(1) == 0)
    def _():
        st_ref[...] = jnp.zeros_like(st_ref)
        row = lax.broadcasted_iota(jnp.int32, (q, q), 0)
        col = lax.broadcasted_iota(jnp.int32, (q, q), 1)
        diff = (row - col).astype(F32)
        rowf = row.astype(F32)
        for h in range(RET_HEADS):
            lg = _RET_LOG_GAMMA[h]
            dm_ref[h] = jnp.exp(jnp.where(diff >= 0, diff * lg, -jnp.inf)) * k_scale
            qd_ref[h] = jnp.exp((rowf + 1.0) * lg)
            kd_ref[h] = jnp.exp((q - 1.0 - col.astype(F32)) * lg) * k_scale

    half = RET_DK // 2
    heads = range(RET_HEADS)

    def front(rows):
        cos = cos_ref[rows, :]
        sin = sin_ref[rows, :]
        cos_t = cos_t_ref[:, rows]
        sin_t = sin_t_ref[:, rows]

        tq = [jnp.dot(proj_ref[rows, RET_DK * h:RET_DK * (h + 1)], perm_ref[...],
                      preferred_element_type=F32) for h in heads]
        tk = [lax.dot_general(perm_t_ref[...], proj_ref[rows, RET_QK_DIM + RET_DK * h:RET_QK_DIM + RET_DK * (h + 1)],
                              (((1,), (1,)), ((), ())), preferred_element_type=F32) for h in heads]
        qb = []
        for t in tq:
            t1, t2 = t[:, :half], t[:, half:]
            qb.append(jnp.concatenate([t1 * cos - t2 * sin, t1 * sin + t2 * cos], axis=1).astype(BF16))
        kr_t = []
        for t in tk:
            t1, t2 = t[:half, :], t[half:, :]
            kr_t.append(jnp.concatenate([t1 * cos_t - t2 * sin_t, t1 * sin_t + t2 * cos_t], axis=0))
        att = [jnp.dot(qb[h], kr_t[h].astype(BF16), preferred_element_type=F32) for h in heads]
        return qb, kr_t, att

    def back(rows, qb, kr_t, att):
        def v_of(h):
            return proj_ref[rows, 2 * RET_QK_DIM + RET_DV * h:2 * RET_QK_DIM + RET_DV * (h + 1)]

        o_st = [jnp.dot(qb[h], st_ref[h].astype(BF16), preferred_element_type=F32) for h in heads]
        for h in heads:
            o_att = jnp.dot((att[h] * dm_ref[h]).astype(BF16), v_of(h), preferred_element_type=F32)
            o = o_att + o_st[h] * jnp.concatenate([qd_ref[h]] * (RET_DV // q), axis=1)
            vs = slice(RET_DV * h, RET_DV * (h + 1))
            gate = proj_ref[rows, 2 * RET_QK_DIM + RET_V_DIM + RET_DV * h:
                            2 * RET_QK_DIM + RET_V_DIM + RET_DV * (h + 1)].astype(F32)
            o_ref[rows, vs] = (_silu(gate) * _rms_rows(o, nw_ref[:, vs])).astype(o_ref.dtype)
            kdec_t = (kr_t[h] * jnp.concatenate([kd_ref[h]] * (RET_DK // q), axis=0)).astype(BF16)
            st_ref[h] = st_ref[h] * math.exp(q * _RET_LOG_GAMMA[h]) + jnp.dot(
                kdec_t, v_of(h), preferred_element_type=F32)

    chunk_rows = [slice(q * c, q * (c + 1)) for c in range(RET_CHUNKS_PER_STEP)]
    ready = front(chunk_rows[0])
    for c in range(RET_CHUNKS_PER_STEP):
        current = ready
        if c + 1 < RET_CHUNKS_PER_STEP:
            ready = front(chunk_rows[c + 1])
        back(chunk_rows[c], *current)


def _retention(proj, cos, sin, norm_w, batch, name):
    m = proj.shape[0]
    rows = CHUNK * RET_CHUNKS_PER_STEP
    nc = m // batch // rows
    row_blk = lambda b, c: (b * nc + c, 0)
    src = jnp.arange(RET_DK)[:, None]
    perm = (jnp.arange(RET_DK)[None, :] == (src % 2) * (RET_DK // 2) + src // 2).astype(BF16)
    return pl.pallas_call(
        _ret_kernel,
        grid=(batch, nc),
        in_specs=[pl.BlockSpec((rows, proj.shape[1]), row_blk),
                  pl.BlockSpec((rows, RET_DK // 2), lambda b, c: (c, 0)),
                  pl.BlockSpec((rows, RET_DK // 2), lambda b, c: (c, 0)),
                  pl.BlockSpec((RET_DK // 2, rows), lambda b, c: (0, c)),
                  pl.BlockSpec((RET_DK // 2, rows), lambda b, c: (0, c)),
                  pl.BlockSpec((RET_DK, RET_DK), lambda b, c: (0, 0)),
                  pl.BlockSpec((RET_DK, RET_DK), lambda b, c: (0, 0)),
                  pl.BlockSpec((1, RET_V_DIM), lambda b, c: (0, 0))],
        out_specs=pl.BlockSpec((rows, RET_V_DIM), row_blk),
        out_shape=jax.ShapeDtypeStruct((m, RET_V_DIM), BF16),
        scratch_shapes=[pltpu.VMEM((RET_HEADS, RET_DK, RET_DV), F32),
                        pltpu.VMEM((RET_HEADS, CHUNK, CHUNK), F32),
                        pltpu.VMEM((RET_HEADS, CHUNK, CHUNK), F32),
                        pltpu.VMEM((RET_HEADS, CHUNK, CHUNK), F32)],
        compiler_params=_params("parallel", "arbitrary"),
        name=name,
    )(proj, cos, sin, cos.T, sin.T, perm, perm.T, norm_w.reshape(1, -1))


def _ffn_hidden(x, nw, wg, wu, layer):
    return _ffn_in(x, nw, wg, wu, layer, 2048, 512, "ffn_in%d" % layer)


def kernel(x, norm_mix, ssm_w_in, ssm_conv_w, ssm_conv_b, ssm_dt_bias, ssm_a_log, ssm_d, ssm_norm, ssm_w_out, ret_w_in, ret_norm, ret_w_out, norm_ffn, ffn_w_gate, ffn_w_up, ffn_w_down, norm_final):
    batch, seq, d = x.shape
    m = batch * seq
    xf = x.reshape(m, d)
    wg, wu, wd = ffn_w_gate.astype(BF16), ffn_w_up.astype(BF16), ffn_w_down.astype(BF16)

    n_zx = SSM_D_INNER + SSM_CONV_DIM
    pad_h = (0, LANES - SSM_HEADS)
    w_dt = jnp.pad(ssm_w_in[0, :, n_zx:], ((0, 0), pad_h)).astype(BF16)
    zx, dt = _norm_matmul(xf, norm_mix[0], [ssm_w_in.astype(BF16)], 0, n_zx, BF16, 1024, 2048, "ssm_in",
                          w_side=w_dt)
    y = _ssd(zx, dt, ssm_conv_w[0], ssm_conv_b[0],
             jnp.pad(ssm_dt_bias[0].astype(F32), pad_h),
             jnp.pad(-jnp.exp(ssm_a_log[0].astype(F32)), pad_h),
             jnp.repeat(ssm_d[0].astype(F32), SSM_HEADDIM), ssm_norm[0], batch, "ssd")
    xf = _matmul_resid(y, ssm_w_out.astype(BF16), 0, xf, 512, "ssm_out")
    hidden = _ffn_hidden(xf, norm_ffn[0], wg, wu, 0)
    xf = _matmul_resid(hidden, wd, 0, xf, 512, "ffn_out0")

    proj = _norm_matmul(xf, norm_mix[1], [ret_w_in.astype(BF16)], 0, ret_w_in.shape[2], BF16, 1024, 2048, "ret_in")
    freq = 1.0 / (ROPE_BASE ** jnp.linspace(0.0, 1.0, RET_DK // 2, dtype=F32))
    ang = jnp.arange(seq, dtype=F32)[:, None] * freq[None, :]
    o = _retention(proj, jnp.cos(ang), jnp.sin(ang), ret_norm[0], batch, "retention")
    xf = _matmul_resid(o, ret_w_out.astype(BF16), 0, xf, 512, "ret_out")
    hidden = _ffn_hidden(xf, norm_ffn[1], wg, wu, 1)
    out = _matmul_resid(hidden, wd, 1, xf, 512, "ffn_out1_norm", norm_w=norm_final)
    return out.reshape(batch, seq, d)
```

```python
import functools
import math

import jax
import jax.numpy as jnp
from jax import lax
from jax.experimental import pallas as pl
from jax.experimental.pallas import tpu as pltpu

F32 = jnp.float32
BF16 = jnp.bfloat16

D_MODEL = 2048
NORM_EPS = 1e-6
CHUNK = 128

SSM_D_INNER = 4096
SSM_HEADDIM = 64
SSM_HEADS = 64
SSM_D_STATE = 128
SSM_GROUPS = 8
SSM_HPG = 8
SSM_CONV = 4
SSM_GN = SSM_GROUPS * SSM_D_STATE
SSM_CONV_DIM = SSM_D_INNER + 2 * SSM_GN
SSM_GROUP_W = SSM_HPG * SSM_HEADDIM
CONV_TAIL = 16
SSD_CHUNKS_PER_STEP = 2
CONV_AHEAD = 2
LOG2E = 1.0 / math.log(2.0)

RET_HEADS = 8
RET_QK_DIM = 2048
RET_V_DIM = 4096
RET_DK = 256
RET_DV = 512
ROPE_BASE = 10000.0
RET_CHUNKS_PER_STEP = 2

LANES = 128
VMEM_LIMIT = 58 * 1024 * 1024


def _params(*sem):
    return pltpu.CompilerParams(dimension_semantics=sem, vmem_limit_bytes=VMEM_LIMIT)


def _silu(x):
    h = 0.5 * x
    return h + h * jnp.tanh(h)


def _rms_rows(x, w):
    ms = jnp.mean(x * x, axis=-1, keepdims=True)
    return x * lax.rsqrt(ms + NORM_EPS) * w


def _w_spec(k, tn, layer):
    return pl.BlockSpec((None, k, tn), lambda i, j: (layer, 0, j))


def _norm_matmul_kernel(x_ref, nw_ref, *refs, n_w, has_side, epilogue):
    w_refs = refs[:n_w]
    refs = refs[n_w:]
    if has_side:
        w2_ref, o_ref, o2_ref, h_ref, inv_ref = refs
    else:
        o_ref, h_ref, inv_ref = refs
    j = pl.program_id(1)

    def emit(h, inv):
        o_ref[...] = epilogue(*[jnp.dot(h, w[...], preferred_element_type=F32) * inv
                                for w in w_refs]).astype(o_ref.dtype)

    @pl.when(j == 0)
    def _():
        x = x_ref[...]
        inv = lax.rsqrt(jnp.mean(x * x, axis=-1, keepdims=True) + NORM_EPS)
        inv_ref[...] = jnp.broadcast_to(inv, inv_ref.shape)
        h = (x * nw_ref[...]).astype(BF16)
        h_ref[...] = h
        emit(h, inv)
        if has_side:
            o2_ref[...] = jnp.dot(h, w2_ref[...], preferred_element_type=F32) * inv

    @pl.when(j > 0)
    def _():
        emit(h_ref[...], inv_ref[:, 0:1])


def _norm_matmul(x, nw, ws, layer, n, out_dtype, tm, tn, name, epilogue=lambda y: y, w_side=None):
    m, k = x.shape
    in_specs = [pl.BlockSpec((tm, k), lambda i, j: (i, 0)),
                pl.BlockSpec((1, k), lambda i, j: (0, 0))] + [_w_spec(k, tn, layer) for _ in ws]
    out_specs = pl.BlockSpec((tm, tn), lambda i, j: (i, j))
    out_shape = jax.ShapeDtypeStruct((m, n), out_dtype)
    operands = [x, nw.reshape(1, k), *ws]
    if w_side is not None:
        n2 = w_side.shape[1]
        in_specs.append(pl.BlockSpec((k, n2), lambda i, j: (0, 0)))
        out_specs = [out_specs, pl.BlockSpec((tm, n2), lambda i, j: (i, 0))]
        out_shape = [out_shape, jax.ShapeDtypeStruct((m, n2), F32)]
        operands.append(w_side)
    return pl.pallas_call(
        functools.partial(_norm_matmul_kernel, n_w=len(ws), has_side=w_side is not None, epilogue=epilogue),
        grid=(m // tm, n // tn),
        in_specs=in_specs,
        out_specs=out_specs,
        out_shape=out_shape,
        scratch_shapes=[pltpu.VMEM((tm, k), BF16), pltpu.VMEM((tm, LANES), F32)],
        compiler_params=_params("parallel", "arbitrary"),
        name=name,
    )(*operands)


def _ffn_in_kernel(x_hbm, nw_ref, wg_ref, wu_ref, o_ref, x_buf, h_ref, inv_ref, x_sem):
    i = pl.program_id(0)
    j = pl.program_id(1)
    tm = x_buf.shape[0]

    def x_copy(tile):
        return pltpu.make_async_copy(x_hbm.at[pl.ds(tile * tm, tm), :], x_buf, x_sem)

    def emit(h, inv):
        g = jnp.dot(h, wg_ref[...], preferred_element_type=F32) * inv
        u = jnp.dot(h, wu_ref[...], preferred_element_type=F32) * inv
        o_ref[...] = (_silu(g) * u).astype(o_ref.dtype)

    @pl.when(jnp.logical_and(i == 0, j == 0))
    def _():
        x_copy(0).start()

    @pl.when(j == 0)
    def _():
        x_copy(i).wait()
        x = x_buf[...]
        inv = lax.rsqrt(jnp.mean(x * x, axis=-1, keepdims=True) + NORM_EPS)
        inv_ref[...] = jnp.broadcast_to(inv, inv_ref.shape)
        h = (x * nw_ref[...]).astype(BF16)
        h_ref[...] = h
        emit(h, inv)

    @pl.when(jnp.logical_and(j == 1, i + 1 < pl.num_programs(0)))
    def _():
        x_copy(i + 1).start()

    @pl.when(j > 0)
    def _():
        emit(h_ref[...], inv_ref[:, 0:1])


def _ffn_in(x, nw, wg, wu, layer, tm, tn, name):
    m, k = x.shape
    n = wg.shape[2]
    assert n // tn >= 2
    return pl.pallas_call(
        _ffn_in_kernel,
        grid=(m // tm, n // tn),
        in_specs=[pl.BlockSpec(memory_space=pl.ANY),
                  pl.BlockSpec((1, k), lambda i, j: (0, 0)),
                  _w_spec(k, tn, layer),
                  _w_spec(k, tn, layer)],
        out_specs=pl.BlockSpec((tm, tn), lambda i, j: (i, j)),
        out_shape=jax.ShapeDtypeStruct((m, n), BF16),
        scratch_shapes=[pltpu.VMEM((tm, k), F32), pltpu.VMEM((tm, k), BF16), pltpu.VMEM((tm, LANES), F32),
                        pltpu.SemaphoreType.DMA(())],
        compiler_params=_params("arbitrary", "arbitrary"),
        name=name,
    )(x, nw.reshape(1, k), wg, wu)


def _matmul_resid_kernel(a_ref, w_ref, r_ref, *refs, has_norm):
    o_ref = refs[-1]
    val = r_ref[...] + jnp.dot(a_ref[...], w_ref[...], preferred_element_type=F32)
    o_ref[...] = _rms_rows(val, refs[0][...]) if has_norm else val


def _matmul_resid(a, w, layer, resid, tm, name, norm_w=None):
    m, k = a.shape
    n = w.shape[2]
    in_specs = [pl.BlockSpec((tm, k), lambda i: (i, 0)),
                pl.BlockSpec((None, k, n), lambda i: (layer, 0, 0), pipeline_mode=pl.Buffered(1)),
                pl.BlockSpec((tm, n), lambda i: (i, 0))]
    operands = [a, w, resid]
    if norm_w is not None:
        in_specs.append(pl.BlockSpec((1, n), lambda i: (0, 0)))
        operands.append(norm_w.reshape(1, n))
    return pl.pallas_call(
        functools.partial(_matmul_resid_kernel, has_norm=norm_w is not None),
        grid=(m // tm,),
        in_specs=in_specs,
        out_specs=pl.BlockSpec((tm, n), lambda i: (i, 0)),
        out_shape=jax.ShapeDtypeStruct((m, n), F32),
        compiler_params=_params("parallel"),
        name=name,
    )(*operands)


def _ssd_kernel(zx_ref, dt_ref, shift_ref, cw_ref, cb_ref, dtb_ref, aneg_ref, dskip_ref, nw_ref,
                y_ref, xpad_ref, st_ref):
    q = CHUNK

    @pl.when(pl.program_id(1) == 0)
    def _():
        xpad_ref[0:CONV_TAIL, :] = jnp.zeros((CONV_TAIL, SSM_CONV_DIM), BF16)
        st_ref[...] = jnp.zeros_like(st_ref)

    xpad_ref[CONV_TAIL:, :] = zx_ref[:, SSM_D_INNER:]

    row = lax.broadcasted_iota(jnp.int32, (q, q), 0)
    col = lax.broadcasted_iota(jnp.int32, (q, q), 1)
    causal = row >= col
    neg_mask = jnp.where(causal, 0.0, -jnp.inf)
    lo = col < SSM_HEADDIM

    for chunk in range(SSD_CHUNKS_PER_STEP):
        _ssd_chunk(q * chunk, zx_ref, dt_ref, shift_ref, cw_ref, cb_ref, dtb_ref, aneg_ref, dskip_ref, nw_ref,
                   y_ref, xpad_ref, st_ref, causal, neg_mask, lo)

    xpad_ref[0:CONV_TAIL, :] = xpad_ref[q * SSD_CHUNKS_PER_STEP:q * SSD_CHUNKS_PER_STEP + CONV_TAIL, :]


def _ssd_chunk(r0, zx_ref, dt_ref, shift_ref, cw_ref, cb_ref, dtb_ref, aneg_ref, dskip_ref, nw_ref,
               y_ref, xpad_ref, st_ref, causal, neg_mask, lo):
    q = CHUNK
    rows = slice(r0, r0 + q)
    dtr = dt_ref[rows, :] + dtb_ref[...]
    dt = jnp.maximum(dtr, 0.0) + jnp.log1p(jnp.exp(-jnp.abs(dtr)))
    a = dt * aneg_ref[...]
    cum = jnp.dot(causal.astype(F32), a, precision=lax.Precision.HIGHEST,
                  preferred_element_type=F32) * LOG2E
    cum_t = cum.T
    dt_t = dt.T
    to_end_t = jnp.exp2(cum_t[:, q - 1:q] - cum_t) * dt_t

    def conv_offsets(g):
        return ((SSM_GROUP_W * g, SSM_GROUP_W),
                (SSM_D_INNER + SSM_D_STATE * g, SSM_D_STATE),
                (SSM_D_INNER + SSM_GN + SSM_D_STATE * g, SSM_D_STATE))

    def conv_shift(g):
        return [jnp.dot(shift_ref[...], xpad_ref[r0:r0 + CONV_TAIL + q, off:off + width], preferred_element_type=F32)
                for off, width in conv_offsets(g)]

    def conv_finish(g, shifted):
        outs = []
        for (off, width), sh in zip(conv_offsets(g), shifted):
            cur = zx_ref[rows, SSM_D_INNER + off:SSM_D_INNER + off + width].astype(F32)
            acc = cb_ref[:, off:off + width] + cw_ref[SSM_CONV - 1:SSM_CONV, off:off + width] * cur
            for k in range(SSM_CONV - 1):
                acc = acc + cw_ref[k:k + 1, off:off + width] * sh[k * q:(k + 1) * q]
            outs.append(_silu(acc))
        return outs

    def prepare(g, shifted):
        xg, bg, cg = conv_finish(g, shifted)
        cb = cg.astype(BF16)
        cbm = lax.dot_general(cb, bg.astype(BF16), (((1,), (1,)), ((), ())),
                              preferred_element_type=F32)
        st = st_ref[g]
        y_inter = jnp.dot(cb, st.astype(BF16), preferred_element_type=F32)
        return xg, bg.T, cbm, st, y_inter

    def scan(g, xg, bg_t, cbm, st, y_inter):
        gx = SSM_GROUP_W * g
        y_pairs = []
        for p in range(SSM_HPG // 2):
            cols = slice(LANES * p, LANES * (p + 1))
            xp = xg[:, cols]
            x2 = jnp.concatenate([jnp.where(lo, xp, 0.0), jnp.where(lo, 0.0, xp)], axis=0).astype(BF16)
            wgts, bts, cis, decs = [], [], [], []
            for hh in (SSM_HPG * g + 2 * p, SSM_HPG * g + 2 * p + 1):
                ci = jnp.broadcast_to(cum[:, hh:hh + 1], (q, q))
                cj = cum_t[hh:hh + 1, :]
                decay = jnp.exp2(ci - cj + neg_mask)
                wgts.append((decay * cbm * dt_t[hh:hh + 1, :]).astype(BF16))
                bts.append((bg_t * to_end_t[hh:hh + 1, :]).astype(BF16))
                cis.append(ci)
                decs.append(jnp.exp2(cum_t[hh:hh + 1, q - 1:q]))
            y_intra = jnp.dot(jnp.concatenate(wgts, axis=1), x2, preferred_element_type=F32)
            y_pairs.append(y_intra + y_inter[:, cols] * jnp.exp2(jnp.where(lo, cis[0], cis[1])))
            d_st = jnp.dot(jnp.concatenate(bts, axis=1), x2, preferred_element_type=F32)
            dec = jnp.where(lo[0:1, :], decs[0], decs[1])
            st_ref[g, :, cols] = st[:, cols] * dec + d_st

        yg = jnp.concatenate(y_pairs, axis=1)
        yg = yg + dskip_ref[:, gx:gx + SSM_GROUP_W] * xg
        yg = yg * _silu(zx_ref[rows, gx:gx + SSM_GROUP_W].astype(F32))
        y_ref[rows, gx:gx + SSM_GROUP_W] = _rms_rows(yg, nw_ref[:, gx:gx + SSM_GROUP_W]).astype(y_ref.dtype)

    shifted = {g: conv_shift(g) for g in range(CONV_AHEAD)}
    ready = prepare(0, shifted.pop(0))
    for g in range(SSM_GROUPS):
        current = ready
        if g + CONV_AHEAD < SSM_GROUPS:
            shifted[g + CONV_AHEAD] = conv_shift(g + CONV_AHEAD)
        if g + 1 < SSM_GROUPS:
            ready = prepare(g + 1, shifted.pop(g + 1))
        scan(g, *current)


def _ssd(zx, dt, conv_w, conv_b, dt_bias, a_neg, d_skip, norm_w, batch, name):
    m = zx.shape[0]
    rows = CHUNK * SSD_CHUNKS_PER_STEP
    nc = m // batch // rows
    row_blk = lambda b, c: (b * nc + c, 0)
    const = lambda b, c: (0, 0)
    out_row = jnp.arange((SSM_CONV - 1) * CHUNK)[:, None]
    src_row = CONV_TAIL + out_row % CHUNK - (SSM_CONV - 1) + out_row // CHUNK
    shift = (jnp.arange(CONV_TAIL + CHUNK)[None, :] == src_row).astype(BF16)
    return pl.pallas_call(
        _ssd_kernel,
        grid=(batch, nc),
        in_specs=[pl.BlockSpec((rows, zx.shape[1]), row_blk),
                  pl.BlockSpec((rows, LANES), row_blk),
                  pl.BlockSpec(((SSM_CONV - 1) * CHUNK, CONV_TAIL + CHUNK), const),
                  pl.BlockSpec((SSM_CONV, SSM_CONV_DIM), const),
                  pl.BlockSpec((1, SSM_CONV_DIM), const),
                  pl.BlockSpec((1, LANES), const),
                  pl.BlockSpec((1, LANES), const),
                  pl.BlockSpec((1, SSM_D_INNER), const),
                  pl.BlockSpec((1, SSM_D_INNER), const)],
        out_specs=pl.BlockSpec((rows, SSM_D_INNER), row_blk),
        out_shape=jax.ShapeDtypeStruct((m, SSM_D_INNER), BF16),
        scratch_shapes=[pltpu.VMEM((CONV_TAIL + rows, SSM_CONV_DIM), BF16),
                        pltpu.VMEM((SSM_GROUPS, SSM_D_STATE, SSM_GROUP_W), F32)],
        compiler_params=_params("parallel", "arbitrary"),
        name=name,
    )(zx, dt, shift, conv_w, conv_b.reshape(1, -1), dt_bias.reshape(1, -1), a_neg.reshape(1, -1),
      d_skip.reshape(1, -1), norm_w.reshape(1, -1))


_RET_LOG_GAMMA = [math.log(1.0 - 2.0 ** (-5.0 - h)) for h in range(RET_HEADS)]


def _ret_kernel(proj_ref, cos_ref, sin_ref, perm_ref, nw_ref, o_ref, st_ref, dm_ref, qd_ref, kd_ref):
    q = CHUNK
    k_scale = RET_DK ** -0.5

    @pl.when(pl.program_id(1) == 0)
    def _():
        st_ref[...] = jnp.zeros_like(st_ref)
        row = lax.broadcasted_iota(jnp.int32, (q, q), 0)
        col = lax.broadcasted_iota(jnp.int32, (q, q), 1)
        diff = (row - col).astype(F32)
        rowf = row.astype(F32)
        for h in range(RET_HEADS):
            lg = _RET_LOG_GAMMA[h]
            dm_ref[h] = jnp.exp(jnp.where(diff >= 0, diff * lg, -jnp.inf)) * k_scale
            qd_ref[h] = jnp.exp((rowf + 1.0) * lg)
            kd_ref[h] = jnp.exp((q - 1.0 - rowf) * lg) * k_scale

    half = RET_DK // 2
    heads = range(RET_HEADS)

    def front(rows):
        cos = cos_ref[rows, :]
        sin = sin_ref[rows, :]

        def deinterleave(off):
            return jnp.dot(proj_ref[rows, off:off + RET_DK], perm_ref[...], preferred_element_type=F32)

        def rotary(t):
            t1, t2 = t[:, :half], t[:, half:]
            return jnp.concatenate([t1 * cos - t2 * sin, t1 * sin + t2 * cos], axis=1)

        tq = [deinterleave(RET_DK * h) for h in heads]
        tk = [deinterleave(RET_QK_DIM + RET_DK * h) for h in heads]
        qr = [rotary(t) for t in tq]
        kr = [rotary(t) for t in tk]
        qb = [t.astype(BF16) for t in qr]
        qs = [(qr[h] * jnp.concatenate([qd_ref[h]] * (RET_DK // q), axis=1)).astype(BF16) for h in heads]
        att = [lax.dot_general(qb[h], kr[h].astype(BF16), (((1,), (1,)), ((), ())),
                               preferred_element_type=F32) for h in heads]
        return qs, kr, att

    def back(rows, qs, kr, att):
        def v_of(h):
            return proj_ref[rows, 2 * RET_QK_DIM + RET_DV * h:2 * RET_QK_DIM + RET_DV * (h + 1)]

        for h in heads:
            lhs = jnp.concatenate([(att[h] * dm_ref[h]).astype(BF16), qs[h]], axis=1)
            rhs = jnp.concatenate([v_of(h), st_ref[h].astype(BF16)], axis=0)
            o = jnp.dot(lhs, rhs, preferred_element_type=F32)
            vs = slice(RET_DV * h, RET_DV * (h + 1))
            gate = proj_ref[rows, 2 * RET_QK_DIM + RET_V_DIM + RET_DV * h:
                            2 * RET_QK_DIM + RET_V_DIM + RET_DV * (h + 1)].astype(F32)
            o_ref[rows, vs] = (_silu(gate) * _rms_rows(o, nw_ref[:, vs])).astype(o_ref.dtype)
            kdec = (kr[h] * jnp.concatenate([kd_ref[h]] * (RET_DK // q), axis=1)).astype(BF16)
            st_ref[h] = st_ref[h] * math.exp(q * _RET_LOG_GAMMA[h]) + lax.dot_general(
                kdec, v_of(h), (((0,), (0,)), ((), ())), preferred_element_type=F32)

    chunk_rows = [slice(q * c, q * (c + 1)) for c in range(RET_CHUNKS_PER_STEP)]
    ready = front(chunk_rows[0])
    for c in range(RET_CHUNKS_PER_STEP):
        current = ready
        if c + 1 < RET_CHUNKS_PER_STEP:
            ready = front(chunk_rows[c + 1])
        back(chunk_rows[c], *current)


def _retention(proj, cos, sin, norm_w, batch, name):
    m = proj.shape[0]
    rows = CHUNK * RET_CHUNKS_PER_STEP
    nc = m // batch // rows
    row_blk = lambda b, c: (b * nc + c, 0)
    src = jnp.arange(RET_DK)[:, None]
    perm = (jnp.arange(RET_DK)[None, :] == (src % 2) * (RET_DK // 2) + src // 2).astype(BF16)
    return pl.pallas_call(
        _ret_kernel,
        grid=(batch, nc),
        in_specs=[pl.BlockSpec((rows, proj.shape[1]), row_blk),
                  pl.BlockSpec((rows, RET_DK // 2), lambda b, c: (c, 0)),
                  pl.BlockSpec((rows, RET_DK // 2), lambda b, c: (c, 0)),
                  pl.BlockSpec((RET_DK, RET_DK), lambda b, c: (0, 0)),
                  pl.BlockSpec((1, RET_V_DIM), lambda b, c: (0, 0))],
        out_specs=pl.BlockSpec((rows, RET_V_DIM), row_blk),
        out_shape=jax.ShapeDtypeStruct((m, RET_V_DIM), BF16),
        scratch_shapes=[pltpu.VMEM((RET_HEADS, RET_DK, RET_DV), F32),
                        pltpu.VMEM((RET_HEADS, CHUNK, CHUNK), F32),
                        pltpu.VMEM((RET_HEADS, CHUNK, CHUNK), F32),
                        pltpu.VMEM((RET_HEADS, CHUNK, CHUNK), F32)],
        compiler_params=_params("parallel", "arbitrary"),
        name=name,
    )(proj, cos, sin, perm, norm_w.reshape(1, -1))


def _ffn_hidden(x, nw, wg, wu, layer):
    return _ffn_in(x, nw, wg, wu, layer, 2048, 512, "ffn_in%d" % layer)


def kernel(x, norm_mix, ssm_w_in, ssm_conv_w, ssm_conv_b, ssm_dt_bias, ssm_a_log, ssm_d, ssm_norm, ssm_w_out, ret_w_in, ret_norm, ret_w_out, norm_ffn, ffn_w_gate, ffn_w_up, ffn_w_down, norm_final):
    batch, seq, d = x.shape
    m = batch * seq
    xf = x.reshape(m, d)
    wg, wu, wd = ffn_w_gate.astype(BF16), ffn_w_up.astype(BF16), ffn_w_down.astype(BF16)

    n_zx = SSM_D_INNER + SSM_CONV_DIM
    pad_h = (0, LANES - SSM_HEADS)
    w_dt = jnp.pad(ssm_w_in[0, :, n_zx:], ((0, 0), pad_h)).astype(BF16)
    zx, dt = _norm_matmul(xf, norm_mix[0], [ssm_w_in.astype(BF16)], 0, n_zx, BF16, 1024, 2048, "ssm_in",
                          w_side=w_dt)
    y = _ssd(zx, dt, ssm_conv_w[0], ssm_conv_b[0],
             jnp.pad(ssm_dt_bias[0].astype(F32), pad_h),
             jnp.pad(-jnp.exp(ssm_a_log[0].astype(F32)), pad_h),
             jnp.repeat(ssm_d[0].astype(F32), SSM_HEADDIM), ssm_norm[0], batch, "ssd")
    xf = _matmul_resid(y, ssm_w_out.astype(BF16), 0, xf, 512, "ssm_out")
    hidden = _ffn_hidden(xf, norm_ffn[0], wg, wu, 0)
    xf = _matmul_resid(hidden, wd, 0, xf, 512, "ffn_out0")

    proj = _norm_matmul(xf, norm_mix[1], [ret_w_in.astype(BF16)], 0, ret_w_in.shape[2], BF16, 1024, 2048, "ret_in")
    freq = 1.0 / (ROPE_BASE ** jnp.linspace(0.0, 1.0, RET_DK // 2, dtype=F32))
    ang = jnp.arange(seq, dtype=F32)[:, None] * freq[None, :]
    o = _retention(proj, jnp.cos(ang), jnp.sin(ang), ret_norm[0], batch, "retention")
    xf = _matmul_resid(o, ret_w_out.astype(BF16), 0, xf, 512, "ret_out")
    hidden = _ffn_hidden(xf, norm_ffn[1], wg, wu, 1)
    out = _matmul_resid(hidden, wd, 1, xf, 512, "ffn_out1_norm", norm_w=norm_final)
    return out.reshape(batch, seq, d)
```

```python
import functools
import math

import jax
import jax.numpy as jnp
from jax import lax
from jax.experimental import pallas as pl
from jax.experimental.pallas import tpu as pltpu

F32 = jnp.float32
BF16 = jnp.bfloat16

D_MODEL = 2048
NORM_EPS = 1e-6
CHUNK = 128

SSM_D_INNER = 4096
SSM_HEADDIM = 64
SSM_HEADS = 64
SSM_D_STATE = 128
SSM_GROUPS = 8
SSM_HPG = 8
SSM_CONV = 4
SSM_GN = SSM_GROUPS * SSM_D_STATE
SSM_CONV_DIM = SSM_D_INNER + 2 * SSM_GN
SSM_GROUP_W = SSM_HPG * SSM_HEADDIM
CONV_TAIL = 16
SSD_CHUNKS_PER_STEP = 2
CONV_AHEAD = 2
LOG2E = 1.0 / math.log(2.0)

RET_HEADS = 8
RET_QK_DIM = 2048
RET_V_DIM = 4096
RET_DK = 256
RET_DV = 512
ROPE_BASE = 10000.0
RET_CHUNKS_PER_STEP = 2
W_RING = 3

LANES = 128
VMEM_LIMIT = 58 * 1024 * 1024


def _params(*sem):
    return pltpu.CompilerParams(dimension_semantics=sem, vmem_limit_bytes=VMEM_LIMIT)


def _silu(x):
    h = 0.5 * x
    return h + h * jnp.tanh(h)


def _rms_rows(x, w):
    ms = jnp.mean(x * x, axis=-1, keepdims=True)
    return x * lax.rsqrt(ms + NORM_EPS) * w


def _w_spec(k, tn, layer):
    return pl.BlockSpec((None, k, tn), lambda i, j: (layer, 0, j))


def _norm_matmul_kernel(x_ref, nw_ref, *refs, n_w, has_side, epilogue):
    w_refs = refs[:n_w]
    refs = refs[n_w:]
    if has_side:
        w2_ref, o_ref, o2_ref, h_ref, inv_ref = refs
    else:
        o_ref, h_ref, inv_ref = refs
    j = pl.program_id(1)

    def emit(h, inv):
        o_ref[...] = epilogue(*[jnp.dot(h, w[...], preferred_element_type=F32) * inv
                                for w in w_refs]).astype(o_ref.dtype)

    @pl.when(j == 0)
    def _():
        x = x_ref[...]
        inv = lax.rsqrt(jnp.mean(x * x, axis=-1, keepdims=True) + NORM_EPS)
        inv_ref[...] = jnp.broadcast_to(inv, inv_ref.shape)
        h = (x * nw_ref[...]).astype(BF16)
        h_ref[...] = h
        emit(h, inv)
        if has_side:
            o2_ref[...] = jnp.dot(h, w2_ref[...], preferred_element_type=F32) * inv

    @pl.when(j > 0)
    def _():
        emit(h_ref[...], inv_ref[:, 0:1])


def _norm_matmul(x, nw, ws, layer, n, out_dtype, tm, tn, name, epilogue=lambda y: y, w_side=None):
    m, k = x.shape
    in_specs = [pl.BlockSpec((tm, k), lambda i, j: (i, 0)),
                pl.BlockSpec((1, k), lambda i, j: (0, 0))] + [_w_spec(k, tn, layer) for _ in ws]
    out_specs = pl.BlockSpec((tm, tn), lambda i, j: (i, j))
    out_shape = jax.ShapeDtypeStruct((m, n), out_dtype)
    operands = [x, nw.reshape(1, k), *ws]
    if w_side is not None:
        n2 = w_side.shape[1]
        in_specs.append(pl.BlockSpec((k, n2), lambda i, j: (0, 0)))
        out_specs = [out_specs, pl.BlockSpec((tm, n2), lambda i, j: (i, 0))]
        out_shape = [out_shape, jax.ShapeDtypeStruct((m, n2), F32)]
        operands.append(w_side)
    return pl.pallas_call(
        functools.partial(_norm_matmul_kernel, n_w=len(ws), has_side=w_side is not None, epilogue=epilogue),
        grid=(m // tm, n // tn),
        in_specs=in_specs,
        out_specs=out_specs,
        out_shape=out_shape,
        scratch_shapes=[pltpu.VMEM((tm, k), BF16), pltpu.VMEM((tm, LANES), F32)],
        compiler_params=_params("parallel", "arbitrary"),
        name=name,
    )(*operands)


def _ring_matmul_kernel(x_ref, nw_ref, w_hbm, o_ref, h_ref, inv_ref, w_buf, w_sem, *, layer, tn):
    i = pl.program_id(0)
    j = pl.program_id(1)
    nj = pl.num_programs(1)
    step = i * nj + j
    last = pl.num_programs(0) * nj - 1

    def w_copy(s):
        col = pl.multiple_of((s % nj) * tn, tn)
        slot = s % W_RING
        return pltpu.make_async_copy(w_hbm.at[layer, :, pl.ds(col, tn)], w_buf.at[slot], w_sem.at[slot])

    @pl.when(step == 0)
    def _():
        w_copy(step).start()
        w_copy(step + 1).start()

    @pl.when(step + W_RING - 1 <= last)
    def _():
        w_copy(step + W_RING - 1).start()

    w_copy(step).wait()

    def emit(h, inv, slot):
        o_ref[...] = (jnp.dot(h, w_buf[slot], preferred_element_type=F32) * inv).astype(o_ref.dtype)

    @pl.when(j == 0)
    def _():
        x = x_ref[...]
        inv = lax.rsqrt(jnp.mean(x * x, axis=-1, keepdims=True) + NORM_EPS)
        inv_ref[...] = jnp.broadcast_to(inv, inv_ref.shape)
        h = (x * nw_ref[...]).astype(BF16)
        h_ref[...] = h
        emit(h, inv, 0)

    for slot in range(W_RING):
        @pl.when(jnp.logical_and(j > 0, j % W_RING == slot))
        def _():
            emit(h_ref[...], inv_ref[:, 0:1], slot)


def _ring_matmul(x, nw, w, layer, out_dtype, tm, tn, name):
    m, k = x.shape
    n = w.shape[2]
    assert (n // tn) % W_RING == 0
    return pl.pallas_call(
        functools.partial(_ring_matmul_kernel, layer=layer, tn=tn),
        grid=(m // tm, n // tn),
        in_specs=[pl.BlockSpec((tm, k), lambda i, j: (i, 0)),
                  pl.BlockSpec((1, k), lambda i, j: (0, 0)),
                  pl.BlockSpec(memory_space=pl.ANY)],
        out_specs=pl.BlockSpec((tm, tn), lambda i, j: (i, j)),
        out_shape=jax.ShapeDtypeStruct((m, n), out_dtype),
        scratch_shapes=[pltpu.VMEM((tm, k), BF16), pltpu.VMEM((tm, LANES), F32),
                        pltpu.VMEM((W_RING, k, tn), BF16), pltpu.SemaphoreType.DMA((W_RING,))],
        compiler_params=_params("arbitrary", "arbitrary"),
        name=name,
    )(x, nw.reshape(1, k), w)


def _ffn_in_kernel(x_hbm, nw_ref, wg_ref, wu_ref, o_ref, x_buf, h_ref, inv_ref, x_sem):
    i = pl.program_id(0)
    j = pl.program_id(1)
    tm = x_buf.shape[0]

    def x_copy(tile):
        return pltpu.make_async_copy(x_hbm.at[pl.ds(tile * tm, tm), :], x_buf, x_sem)

    def emit(h, inv):
        g = jnp.dot(h, wg_ref[...], preferred_element_type=F32) * inv
        u = jnp.dot(h, wu_ref[...], preferred_element_type=F32) * inv
        o_ref[...] = (_silu(g) * u).astype(o_ref.dtype)

    @pl.when(jnp.logical_and(i == 0, j == 0))
    def _():
        x_copy(0).start()

    @pl.when(j == 0)
    def _():
        x_copy(i).wait()
        x = x_buf[...]
        inv = lax.rsqrt(jnp.mean(x * x, axis=-1, keepdims=True) + NORM_EPS)
        inv_ref[...] = jnp.broadcast_to(inv, inv_ref.shape)
        h = (x * nw_ref[...]).astype(BF16)
        h_ref[...] = h
        emit(h, inv)

    @pl.when(jnp.logical_and(j == 1, i + 1 < pl.num_programs(0)))
    def _():
        x_copy(i + 1).start()

    @pl.when(j > 0)
    def _():
        emit(h_ref[...], inv_ref[:, 0:1])


def _ffn_in(x, nw, wg, wu, layer, tm, tn, name):
    m, k = x.shape
    n = wg.shape[2]
    assert n // tn >= 2
    return pl.pallas_call(
        _ffn_in_kernel,
        grid=(m // tm, n // tn),
        in_specs=[pl.BlockSpec(memory_space=pl.ANY),
                  pl.BlockSpec((1, k), lambda i, j: (0, 0)),
                  _w_spec(k, tn, layer),
                  _w_spec(k, tn, layer)],
        out_specs=pl.BlockSpec((tm, tn), lambda i, j: (i, j)),
        out_shape=jax.ShapeDtypeStruct((m, n), BF16),
        scratch_shapes=[pltpu.VMEM((tm, k), F32), pltpu.VMEM((tm, k), BF16), pltpu.VMEM((tm, LANES), F32),
                        pltpu.SemaphoreType.DMA(())],
        compiler_params=_params("arbitrary", "arbitrary"),
        name=name,
    )(x, nw.reshape(1, k), wg, wu)


def _matmul_resid_kernel(a_ref, w_ref, r_ref, *refs, has_norm):
    o_ref = refs[-1]
    val = r_ref[...] + jnp.dot(a_ref[...], w_ref[...], preferred_element_type=F32)
    o_ref[...] = _rms_rows(val, refs[0][...]) if has_norm else val


def _matmul_resid(a, w, layer, resid, tm, name, norm_w=None):
    m, k = a.shape
    n = w.shape[2]
    in_specs = [pl.BlockSpec((tm, k), lambda i: (i, 0)),
                pl.BlockSpec((None, k, n), lambda i: (layer, 0, 0), pipeline_mode=pl.Buffered(1)),
                pl.BlockSpec((tm, n), lambda i: (i, 0))]
    operands = [a, w, resid]
    if norm_w is not None:
        in_specs.append(pl.BlockSpec((1, n), lambda i: (0, 0)))
        operands.append(norm_w.reshape(1, n))
    return pl.pallas_call(
        functools.partial(_matmul_resid_kernel, has_norm=norm_w is not None),
        grid=(m // tm,),
        in_specs=in_specs,
        out_specs=pl.BlockSpec((tm, n), lambda i: (i, 0)),
        out_shape=jax.ShapeDtypeStruct((m, n), F32),
        compiler_params=_params("parallel"),
        name=name,
    )(*operands)


def _ssd_kernel(zx_ref, dt_ref, shift_ref, cw_ref, cb_ref, dtb_ref, aneg_ref, dskip_ref, nw_ref,
                y_ref, xpad_ref, st_ref):
    q = CHUNK

    @pl.when(pl.program_id(1) == 0)
    def _():
        xpad_ref[0:CONV_TAIL, :] = jnp.zeros((CONV_TAIL, SSM_CONV_DIM), BF16)
        st_ref[...] = jnp.zeros_like(st_ref)

    xpad_ref[CONV_TAIL:, :] = zx_ref[:, SSM_D_INNER:]

    row = lax.broadcasted_iota(jnp.int32, (q, q), 0)
    col = lax.broadcasted_iota(jnp.int32, (q, q), 1)
    causal = row >= col
    neg_mask = jnp.where(causal, 0.0, -jnp.inf)
    lo = col < SSM_HEADDIM

    for chunk in range(SSD_CHUNKS_PER_STEP):
        _ssd_chunk(q * chunk, zx_ref, dt_ref, shift_ref, cw_ref, cb_ref, dtb_ref, aneg_ref, dskip_ref, nw_ref,
                   y_ref, xpad_ref, st_ref, causal, neg_mask, lo)

    xpad_ref[0:CONV_TAIL, :] = xpad_ref[q * SSD_CHUNKS_PER_STEP:q * SSD_CHUNKS_PER_STEP + CONV_TAIL, :]


def _ssd_chunk(r0, zx_ref, dt_ref, shift_ref, cw_ref, cb_ref, dtb_ref, aneg_ref, dskip_ref, nw_ref,
               y_ref, xpad_ref, st_ref, causal, neg_mask, lo):
    q = CHUNK
    rows = slice(r0, r0 + q)
    dtr = dt_ref[rows, :] + dtb_ref[...]
    dt = jnp.maximum(dtr, 0.0) + jnp.log1p(jnp.exp(-jnp.abs(dtr)))
    a = dt * aneg_ref[...]
    cum = jnp.dot(causal.astype(F32), a, precision=lax.Precision.HIGHEST,
                  preferred_element_type=F32) * LOG2E
    cum_t = cum.T
    dt_t = dt.T
    to_end_t = jnp.exp2(cum_t[:, q - 1:q] - cum_t) * dt_t

    def conv_offsets(g):
        return ((SSM_GROUP_W * g, SSM_GROUP_W),
                (SSM_D_INNER + SSM_D_STATE * g, SSM_D_STATE),
                (SSM_D_INNER + SSM_GN + SSM_D_STATE * g, SSM_D_STATE))

    def conv_shift(g):
        return [jnp.dot(shift_ref[...], xpad_ref[r0:r0 + CONV_TAIL + q, off:off + width], preferred_element_type=F32)
                for off, width in conv_offsets(g)]

    def conv_finish(g, shifted):
        outs = []
        for (off, width), sh in zip(conv_offsets(g), shifted):
            cur = zx_ref[rows, SSM_D_INNER + off:SSM_D_INNER + off + width].astype(F32)
            acc = cb_ref[:, off:off + width] + cw_ref[SSM_CONV - 1:SSM_CONV, off:off + width] * cur
            for k in range(SSM_CONV - 1):
                acc = acc + cw_ref[k:k + 1, off:off + width] * sh[k * q:(k + 1) * q]
            outs.append(_silu(acc))
        return outs

    def prepare(g, shifted):
        xg, bg, cg = conv_finish(g, shifted)
        cb = cg.astype(BF16)
        cbm = lax.dot_general(cb, bg.astype(BF16), (((1,), (1,)), ((), ())),
                              preferred_element_type=F32)
        st = st_ref[g]
        y_inter = jnp.dot(cb, st.astype(BF16), preferred_element_type=F32)
        return xg, bg.T, cbm, st, y_inter

    def scan(g, xg, bg_t, cbm, st, y_inter):
        gx = SSM_GROUP_W * g
        y_pairs = []
        for p in range(SSM_HPG // 2):
            cols = slice(LANES * p, LANES * (p + 1))
            xp = xg[:, cols]
            x2 = jnp.concatenate([jnp.where(lo, xp, 0.0), jnp.where(lo, 0.0, xp)], axis=0).astype(BF16)
            wgts, bts, cis, decs = [], [], [], []
            for hh in (SSM_HPG * g + 2 * p, SSM_HPG * g + 2 * p + 1):
                ci = jnp.broadcast_to(cum[:, hh:hh + 1], (q, q))
                cj = cum_t[hh:hh + 1, :]
                decay = jnp.exp2(ci - cj + neg_mask)
                wgts.append((decay * cbm * dt_t[hh:hh + 1, :]).astype(BF16))
                bts.append((bg_t * to_end_t[hh:hh + 1, :]).astype(BF16))
                cis.append(ci)
                decs.append(jnp.exp2(cum_t[hh:hh + 1, q - 1:q]))
            y_intra = jnp.dot(jnp.concatenate(wgts, axis=1), x2, preferred_element_type=F32)
            y_pairs.append(y_intra + y_inter[:, cols] * jnp.exp2(jnp.where(lo, cis[0], cis[1])))
            d_st = jnp.dot(jnp.concatenate(bts, axis=1), x2, preferred_element_type=F32)
            dec = jnp.where(lo[0:1, :], decs[0], decs[1])
            st_ref[g, :, cols] = st[:, cols] * dec + d_st

        yg = jnp.concatenate(y_pairs, axis=1)
        yg = yg + dskip_ref[:, gx:gx + SSM_GROUP_W] * xg
        yg = yg * _silu(zx_ref[rows, gx:gx + SSM_GROUP_W].astype(F32))
        y_ref[rows, gx:gx + SSM_GROUP_W] = _rms_rows(yg, nw_ref[:, gx:gx + SSM_GROUP_W]).astype(y_ref.dtype)

    shifted = {g: conv_shift(g) for g in range(CONV_AHEAD)}
    ready = prepare(0, shifted.pop(0))
    for g in range(SSM_GROUPS):
        current = ready
        if g + CONV_AHEAD < SSM_GROUPS:
            shifted[g + CONV_AHEAD] = conv_shift(g + CONV_AHEAD)
        if g + 1 < SSM_GROUPS:
            ready = prepare(g + 1, shifted.pop(g + 1))
        scan(g, *current)


def _ssd(zx, dt, conv_w, conv_b, dt_bias, a_neg, d_skip, norm_w, batch, name):
    m = zx.shape[0]
    rows = CHUNK * SSD_CHUNKS_PER_STEP
    nc = m // batch // rows
    row_blk = lambda b, c: (b * nc + c, 0)
    const = lambda b, c: (0, 0)
    out_row = jnp.arange((SSM_CONV - 1) * CHUNK)[:, None]
    src_row = CONV_TAIL + out_row % CHUNK - (SSM_CONV - 1) + out_row // CHUNK
    shift = (jnp.arange(CONV_TAIL + CHUNK)[None, :] == src_row).astype(BF16)
    return pl.pallas_call(
        _ssd_kernel,
        grid=(batch, nc),
        in_specs=[pl.BlockSpec((rows, zx.shape[1]), row_blk),
                  pl.BlockSpec((rows, LANES), row_blk),
                  pl.BlockSpec(((SSM_CONV - 1) * CHUNK, CONV_TAIL + CHUNK), const),
                  pl.BlockSpec((SSM_CONV, SSM_CONV_DIM), const),
                  pl.BlockSpec((1, SSM_CONV_DIM), const),
                  pl.BlockSpec((1, LANES), const),
                  pl.BlockSpec((1, LANES), const),
                  pl.BlockSpec((1, SSM_D_INNER), const),
                  pl.BlockSpec((1, SSM_D_INNER), const)],
        out_specs=pl.BlockSpec((rows, SSM_D_INNER), row_blk),
        out_shape=jax.ShapeDtypeStruct((m, SSM_D_INNER), BF16),
        scratch_shapes=[pltpu.VMEM((CONV_TAIL + rows, SSM_CONV_DIM), BF16),
                        pltpu.VMEM((SSM_GROUPS, SSM_D_STATE, SSM_GROUP_W), F32)],
        compiler_params=_params("parallel", "arbitrary"),
        name=name,
    )(zx, dt, shift, conv_w, conv_b.reshape(1, -1), dt_bias.reshape(1, -1), a_neg.reshape(1, -1),
      d_skip.reshape(1, -1), norm_w.reshape(1, -1))


_RET_LOG_GAMMA = [math.log(1.0 - 2.0 ** (-5.0 - h)) for h in range(RET_HEADS)]


def _ret_kernel(proj_ref, cos_ref, sin_ref, perm_ref, nw_ref, o_ref, st_ref, dm_ref, qd_ref, kd_ref):
    q = CHUNK
    k_scale = RET_DK ** -0.5

    @pl.when(pl.program_id(1) == 0)
    def _():
        st_ref[...] = jnp.zeros_like(st_ref)
        row = lax.broadcasted_iota(jnp.int32, (q, q), 0)
        col = lax.broadcasted_iota(jnp.int32, (q, q), 1)
        diff = (row - col).astype(F32)
        rowf = row.astype(F32)
        for h in range(RET_HEADS):
            lg = _RET_LOG_GAMMA[h]
            dm_ref[h] = jnp.exp(jnp.where(diff >= 0, diff * lg, -jnp.inf)) * k_scale
            qd_ref[h] = jnp.exp((rowf + 1.0) * lg)
            kd_ref[h] = jnp.exp((q - 1.0 - rowf) * lg) * k_scale

    half = RET_DK // 2
    heads = range(RET_HEADS)

    def front(rows):
        cos = cos_ref[rows, :]
        sin = sin_ref[rows, :]

        def deinterleave(off):
            return jnp.dot(proj_ref[rows, off:off + RET_DK], perm_ref[...], preferred_element_type=F32)

        def rotary(t):
            t1, t2 = t[:, :half], t[:, half:]
            return jnp.concatenate([t1 * cos - t2 * sin, t1 * sin + t2 * cos], axis=1)

        tq = [deinterleave(RET_DK * h) for h in heads]
        tk = [deinterleave(RET_QK_DIM + RET_DK * h) for h in heads]
        qr = [rotary(t) for t in tq]
        kr = [rotary(t) for t in tk]
        qb = [t.astype(BF16) for t in qr]
        qs = [(qr[h] * jnp.concatenate([qd_ref[h]] * (RET_DK // q), axis=1)).astype(BF16) for h in heads]
        att = [lax.dot_general(qb[h], kr[h].astype(BF16), (((1,), (1,)), ((), ())),
                               preferred_element_type=F32) for h in heads]
        return qs, kr, att

    def back(rows, qs, kr, att):
        def v_of(h):
            return proj_ref[rows, 2 * RET_QK_DIM + RET_DV * h:2 * RET_QK_DIM + RET_DV * (h + 1)]

        for h in heads:
            lhs = jnp.concatenate([(att[h] * dm_ref[h]).astype(BF16), qs[h]], axis=1)
            rhs = jnp.concatenate([v_of(h), st_ref[h].astype(BF16)], axis=0)
            o = jnp.dot(lhs, rhs, preferred_element_type=F32)
            vs = slice(RET_DV * h, RET_DV * (h + 1))
            gate = proj_ref[rows, 2 * RET_QK_DIM + RET_V_DIM + RET_DV * h:
                            2 * RET_QK_DIM + RET_V_DIM + RET_DV * (h + 1)].astype(F32)
            o_ref[rows, vs] = (_silu(gate) * _rms_rows(o, nw_ref[:, vs])).astype(o_ref.dtype)
            kdec = (kr[h] * jnp.concatenate([kd_ref[h]] * (RET_DK // q), axis=1)).astype(BF16)
            st_ref[h] = st_ref[h] * math.exp(q * _RET_LOG_GAMMA[h]) + lax.dot_general(
                kdec, v_of(h), (((0,), (0,)), ((), ())), preferred_element_type=F32)

    chunk_rows = [slice(q * c, q * (c + 1)) for c in range(RET_CHUNKS_PER_STEP)]
    ready = front(chunk_rows[0])
    for c in range(RET_CHUNKS_PER_STEP):
        current = ready
        if c + 1 < RET_CHUNKS_PER_STEP:
            ready = front(chunk_rows[c + 1])
        back(chunk_rows[c], *current)


def _retention(proj, cos, sin, norm_w, batch, name):
    m = proj.shape[0]
    rows = CHUNK * RET_CHUNKS_PER_STEP
    nc = m // batch // rows
    row_blk = lambda b, c: (b * nc + c, 0)
    src = jnp.arange(RET_DK)[:, None]
    perm = (jnp.arange(RET_DK)[None, :] == (src % 2) * (RET_DK // 2) + src // 2).astype(BF16)
    return pl.pallas_call(
        _ret_kernel,
        grid=(batch, nc),
        in_specs=[pl.BlockSpec((rows, proj.shape[1]), row_blk),
                  pl.BlockSpec((rows, RET_DK // 2), lambda b, c: (c, 0)),
                  pl.BlockSpec((rows, RET_DK // 2), lambda b, c: (c, 0)),
                  pl.BlockSpec((RET_DK, RET_DK), lambda b, c: (0, 0)),
                  pl.BlockSpec((1, RET_V_DIM), lambda b, c: (0, 0))],
        out_specs=pl.BlockSpec((rows, RET_V_DIM), row_blk),
        out_shape=jax.ShapeDtypeStruct((m, RET_V_DIM), BF16),
        scratch_shapes=[pltpu.VMEM((RET_HEADS, RET_DK, RET_DV), F32),
                        pltpu.VMEM((RET_HEADS, CHUNK, CHUNK), F32),
                        pltpu.VMEM((RET_HEADS, CHUNK, CHUNK), F32),
                        pltpu.VMEM((RET_HEADS, CHUNK, CHUNK), F32)],
        compiler_params=_params("parallel", "arbitrary"),
        name=name,
    )(proj, cos, sin, perm, norm_w.reshape(1, -1))


def _ffn_hidden(x, nw, wg, wu, layer):
    return _ffn_in(x, nw, wg, wu, layer, 2048, 512, "ffn_in%d" % layer)


def kernel(x, norm_mix, ssm_w_in, ssm_conv_w, ssm_conv_b, ssm_dt_bias, ssm_a_log, ssm_d, ssm_norm, ssm_w_out, ret_w_in, ret_norm, ret_w_out, norm_ffn, ffn_w_gate, ffn_w_up, ffn_w_down, norm_final):
    batch, seq, d = x.shape
    m = batch * seq
    xf = x.reshape(m, d)
    wg, wu, wd = ffn_w_gate.astype(BF16), ffn_w_up.astype(BF16), ffn_w_down.astype(BF16)

    n_zx = SSM_D_INNER + SSM_CONV_DIM
    pad_h = (0, LANES - SSM_HEADS)
    w_dt = jnp.pad(ssm_w_in[0, :, n_zx:], ((0, 0), pad_h)).astype(BF16)
    zx, dt = _norm_matmul(xf, norm_mix[0], [ssm_w_in.astype(BF16)], 0, n_zx, BF16, 1024, 2048, "ssm_in",
                          w_side=w_dt)
    y = _ssd(zx, dt, ssm_conv_w[0], ssm_conv_b[0],
             jnp.pad(ssm_dt_bias[0].astype(F32), pad_h),
             jnp.pad(-jnp.exp(ssm_a_log[0].astype(F32)), pad_h),
             jnp.repeat(ssm_d[0].astype(F32), SSM_HEADDIM), ssm_norm[0], batch, "ssd")
    xf = _matmul_resid(y, ssm_w_out.astype(BF16), 0, xf, 512, "ssm_out")
    hidden = _ffn_hidden(xf, norm_ffn[0], wg, wu, 0)
    xf = _matmul_resid(hidden, wd, 0, xf, 512, "ffn_out0")

    proj = _ring_matmul(xf, norm_mix[1], ret_w_in.astype(BF16), 0, BF16, 1024, 2048, "ret_in")
    freq = 1.0 / (ROPE_BASE ** jnp.linspace(0.0, 1.0, RET_DK // 2, dtype=F32))
    ang = jnp.arange(seq, dtype=F32)[:, None] * freq[None, :]
    o = _retention(proj, jnp.cos(ang), jnp.sin(ang), ret_norm[0], batch, "retention")
    xf = _matmul_resid(o, ret_w_out.astype(BF16), 0, xf, 512, "ret_out")
    hidden = _ffn_hidden(xf, norm_ffn[1], wg, wu, 1)
    out = _matmul_resid(hidden, wd, 1, xf, 512, "ffn_out1_norm", norm_w=norm_final)
    return out.reshape(batch, seq, d)
```
